```python
import math
import jax, jax.numpy as jnp
from jax import lax
import numpy as np

D_MODEL = 1024
BATCH = 8
SEQ = 4096
DEPTH = 4
DEC_BATCH = 8
DEC_SEQ = 8192
PAST_LEN = 128

GDN_HEADS = 4
GDN_DK = 128
GDN_DV = 128
GDN_CONV = 5
GDN_CHUNK = 64
GDN_Q_W = GDN_HEADS * GDN_DK
GDN_V_W = GDN_HEADS * GDN_DV
GDN_QKV_W = 2 * GDN_Q_W + GDN_V_W
HG_HEADS = 4
HG_DK = 128
HG_DV = 128
HG_CHUNK = 16
HG_K_W = HG_HEADS * HG_DK
HG_V_W = HG_HEADS * HG_DV
D_FF = -(-8 * D_MODEL // (3 * 256)) * 256
NORM_EPS = 1e-6

IN_SIZES = (GDN_QKV_W, GDN_V_W, 2 * GDN_HEADS, 2 * GDN_HEADS, HG_K_W, 2 * HG_K_W, HG_V_W, HG_V_W, 2 * D_MODEL)
IN_WIDTH = sum(IN_SIZES)
IN_SPLITS = [sum(IN_SIZES[:i + 1]) for i in range(len(IN_SIZES) - 1)]

kernel_name = 'hybrid_gdn_hgrn2_bidir_encoder'


def rms_norm(x, w):
    xf = x.astype(jnp.float32)
    y = xf * lax.rsqrt(jnp.mean(xf * xf, axis=-1, keepdims=True) + NORM_EPS)
    return (y * w.astype(jnp.float32)).astype(x.dtype)


def l2norm(x):
    return x * lax.rsqrt(jnp.sum(x * x, axis=-1, keepdims=True) + NORM_EPS)


def centred_dwconv(x, w):
    k = w.shape[0]
    return lax.conv_general_dilated(x, w[:, None, :].astype(x.dtype), window_strides=(1,),
                                    padding=[(k // 2, k // 2)],
                                    dimension_numbers=('NWC', 'WIO', 'NWC'),
                                    feature_group_count=x.shape[-1])


def to_heads(t, n_heads):
    b, l, _ = t.shape
    return t.reshape(b, l, n_heads, -1).transpose(0, 2, 1, 3)


def flip_seq(t):
    return jnp.flip(t, axis=2)


def gated_delta_chunked(q, k, v, g, beta):
    b_, h_, l_, dk = q.shape
    dv = v.shape[-1]
    c = GDN_CHUNK
    n = l_ // c
    q = q.reshape(b_, h_, n, c, dk)
    k = k.reshape(b_, h_, n, c, dk)
    v = v.reshape(b_, h_, n, c, dv)
    g = g.reshape(b_, h_, n, c)
    beta = beta.reshape(b_, h_, n, c)
    cum = jnp.cumsum(g, axis=-1)
    causal = jnp.tril(jnp.ones((c, c), bool))
    strict = jnp.tril(jnp.ones((c, c), bool), -1)
    decay = jnp.exp(jnp.where(causal, cum[..., :, None] - cum[..., None, :], -jnp.inf))
    kb = k * beta[..., None]
    a_mat = jnp.where(strict, jnp.einsum('bhncd,bhnsd->bhncs', kb, k) * decay, 0.0)
    rhs = jnp.concatenate([v * beta[..., None], kb * jnp.exp(cum)[..., None]], axis=-1)
    sol = lax.linalg.triangular_solve(a_mat, rhs, left_side=True, lower=True, unit_diagonal=True)
    u, w = sol[..., :dv], sol[..., dv:]
    qk = jnp.einsum('bhncd,bhnsd->bhncs', q, k) * decay
    q_dec = q * jnp.exp(cum)[..., None]
    k_dec = k * jnp.exp(cum[..., -1:] - cum)[..., None]
    last = jnp.exp(cum[..., -1])

    def step(s, inp):
        qk_n, qd_n, kd_n, u_n, w_n, last_n = inp
        v_new = u_n - jnp.einsum('bhcd,bhde->bhce', w_n, s)
        o = jnp.einsum('bhcd,bhde->bhce', qd_n, s) + jnp.einsum('bhcs,bhse->bhce', qk_n, v_new)
        s = s * last_n[..., None, None] + jnp.einsum('bhcd,bhce->bhde', kd_n, v_new)
        return s, o

    xs = (jnp.moveaxis(qk, 2, 0), jnp.moveaxis(q_dec, 2, 0), jnp.moveaxis(k_dec, 2, 0),
          jnp.moveaxis(u, 2, 0), jnp.moveaxis(w, 2, 0), jnp.moveaxis(last, 2, 0))
    s0 = jnp.zeros((b_, h_, dk, dv), q.dtype)
    _, o = lax.scan(step, s0, xs)
    return jnp.moveaxis(o, 0, 2).reshape(b_, h_, l_, dv)


def gla_chunked(q, k, v, logf):
    b_, h_, l_, dk = q.shape
    dv = v.shape[-1]
    c = HG_CHUNK
    n = l_ // c
    q = q.reshape(b_, h_, n, c, dk)
    k = k.reshape(b_, h_, n, c, dk)
    logf = logf.reshape(b_, h_, n, c, dk)
    v = v.reshape(b_, h_, n, c, dv)
    cum = jnp.cumsum(logf, axis=3)
    q_dec = q * jnp.exp(cum)
    k_inv = k * jnp.exp(-cum)
    k_dec = k * jnp.exp(cum[..., -1:, :] - cum)
    last = jnp.exp(cum[..., -1, :])
    causal = jnp.tril(jnp.ones((c, c), bool))
    attn = jnp.where(causal, jnp.einsum('bhncd,bhnsd->bhncs', q_dec, k_inv), 0.0)
    o_intra = jnp.einsum('bhncs,bhnse->bhnce', attn, v)

    def step(s, inp):
        qd_n, kd_n, v_n, last_n = inp
        o = jnp.einsum('bhcd,bhde->bhce', qd_n, s)
        s = s * last_n[..., None] + jnp.einsum('bhcd,bhce->bhde', kd_n, v_n)
        return s, o

    xs = (jnp.moveaxis(q_dec, 2, 0), jnp.moveaxis(k_dec, 2, 0), jnp.moveaxis(v, 2, 0), jnp.moveaxis(last, 2, 0))
    s0 = jnp.zeros((b_, h_, dk, dv), q.dtype)
    _, o_inter = lax.scan(step, s0, xs)
    return (o_intra + jnp.moveaxis(o_inter, 0, 2)).reshape(b_, h_, l_, dv)


def hybrid_layer(x, w_in, conv_w, a_log, dt_bias, gdn_norm_w, lb, hg_norm_w, w_br_gdn, w_br_hg, w_out,
                 n_pre_mix, n_post_mix, n_pre_ffn, n_post_ffn, w_gate, w_up, w_down):
    bsz, seqlen, _ = x.shape
    f32 = jnp.float32
    h = rms_norm(x, n_pre_mix)
    proj = h @ w_in
    qkv_g, z_g, a_g, b_g, q_h, f_h, i_h, g_h, gate_raw = jnp.split(proj, IN_SPLITS, axis=-1)

    qkv = jax.nn.silu(centred_dwconv(qkv_g, conv_w).astype(f32))
    q, k, v = jnp.split(qkv, [GDN_Q_W, 2 * GDN_Q_W], axis=-1)
    q = l2norm(to_heads(q, GDN_HEADS)) * (GDN_DK ** -0.5)
    k = l2norm(to_heads(k, GDN_HEADS))
    v = to_heads(v, GDN_HEADS)
    a = a_g.astype(f32).reshape(bsz, seqlen, 2, GDN_HEADS)
    g = -jnp.exp(a_log.astype(f32)) * jax.nn.softplus(a + dt_bias.astype(f32))
    g = jnp.transpose(g, (2, 0, 3, 1))
    beta = jnp.transpose(jax.nn.sigmoid(b_g.astype(f32).reshape(bsz, seqlen, 2, GDN_HEADS)), (2, 0, 3, 1))
    o_a = gated_delta_chunked(q, k, v, g[0], beta[0]) + flip_seq(
        gated_delta_chunked(flip_seq(q), flip_seq(k), flip_seq(v), flip_seq(g[1]), flip_seq(beta[1])))
    o_a = o_a.transpose(0, 2, 1, 3)
    o_a = rms_norm(o_a, gdn_norm_w) * jax.nn.silu(z_g.astype(f32).reshape(bsz, seqlen, GDN_HEADS, GDN_DV))
    y_a = o_a.reshape(bsz, seqlen, GDN_V_W).astype(x.dtype) @ w_br_gdn

    qh = to_heads(jax.nn.silu(q_h.astype(f32)), HG_HEADS) * (HG_DK ** -0.5)
    lbf = lb.astype(f32)
    f = lbf + (1.0 - lbf) * jax.nn.sigmoid(f_h.astype(f32).reshape(bsz, seqlen, 2, HG_K_W))
    logf = jnp.transpose(jnp.log(f).reshape(bsz, seqlen, 2, HG_HEADS, HG_DK), (2, 0, 3, 1, 4))
    kh = jnp.transpose((1.0 - f).reshape(bsz, seqlen, 2, HG_HEADS, HG_DK), (2, 0, 3, 1, 4))
    ih = to_heads(i_h.astype(f32), HG_HEADS)
    o_b = gla_chunked(qh, kh[0], ih, logf[0]) + flip_seq(
        gla_chunked(flip_seq(qh), flip_seq(kh[1]), flip_seq(ih), flip_seq(logf[1])))
    o_b = o_b.transpose(0, 2, 1, 3)
    o_b = rms_norm(o_b, hg_norm_w) * jax.nn.silu(g_h.astype(f32).reshape(bsz, seqlen, HG_HEADS, HG_DV))
    y_b = o_b.reshape(bsz, seqlen, HG_V_W).astype(x.dtype) @ w_br_hg

    gates = jax.nn.sigmoid(gate_raw.astype(f32)).reshape(bsz, seqlen, 2, D_MODEL)
    merged = (gates[:, :, 0] * y_a.astype(f32) + gates[:, :, 1] * y_b.astype(f32)).astype(x.dtype)
    x = x + rms_norm(merged @ w_out, n_post_mix)

    h2 = rms_norm(x, n_pre_ffn)
    ff = (jax.nn.silu(h2 @ w_gate) * (h2 @ w_up)) @ w_down
    return x + rms_norm(ff, n_post_ffn)


def setup_inputs(seed: int = 0) -> dict:
    key = jax.random.key(seed)
    ks = jax.random.split(key, 24)
    f32 = jnp.float32

    def nrm(k, shape, fan_in):
        return jax.random.normal(k, shape, f32) * (fan_in ** -0.5)

    def gain(k, shape):
        return 1.0 + 0.05 * jax.random.normal(k, shape, f32)

    dt = jnp.exp(jax.random.uniform(ks[5], (DEPTH, 2, GDN_HEADS), f32, math.log(1e-3), math.log(1e-1)))
    return {
        'x_prompt': jax.random.normal(ks[0], (BATCH, SEQ, D_MODEL), f32),
        'x_sample': jax.random.normal(ks[1], (DEC_BATCH, DEC_SEQ, D_MODEL), f32),
        'w_in': nrm(ks[2], (DEPTH, D_MODEL, IN_WIDTH), D_MODEL),
        'conv_w': nrm(ks[3], (DEPTH, GDN_CONV, GDN_QKV_W), GDN_CONV),
        'gdn_a_log': jnp.log(jax.random.uniform(ks[4], (DEPTH, 2, GDN_HEADS), f32, 1.0, 16.0)),
        'gdn_dt_bias': dt + jnp.log(-jnp.expm1(-dt)),
        'gdn_norm_w': gain(ks[6], (DEPTH, GDN_DV)),
        'hgrn_lb_logits': 0.5 * jax.random.normal(ks[7], (DEPTH, 2, HG_K_W), f32),
        'hgrn_norm_w': gain(ks[8], (DEPTH, HG_DV)),
        'w_branch_gdn': nrm(ks[9], (DEPTH, GDN_V_W, D_MODEL), GDN_V_W),
        'w_branch_hgrn': nrm(ks[10], (DEPTH, HG_V_W, D_MODEL), HG_V_W),
        'w_out': nrm(ks[11], (DEPTH, D_MODEL, D_MODEL), D_MODEL),
        'norm_pre_mix': gain(ks[12], (DEPTH, D_MODEL)),
        'norm_post_mix': gain(ks[13], (DEPTH, D_MODEL)),
        'norm_pre_ffn': gain(ks[14], (DEPTH, D_MODEL)),
        'norm_post_ffn': gain(ks[15], (DEPTH, D_MODEL)),
        'w_ffn_gate': nrm(ks[16], (DEPTH, D_MODEL, D_FF), D_MODEL),
        'w_ffn_up': nrm(ks[17], (DEPTH, D_MODEL, D_FF), D_MODEL),
        'w_ffn_down': nrm(ks[18], (DEPTH, D_FF, D_MODEL), D_FF),
    }


def reference(x_prompt, x_sample, w_in, conv_w, gdn_a_log, gdn_dt_bias, gdn_norm_w, hgrn_lb_logits,
              hgrn_norm_w, w_branch_gdn, w_branch_hgrn, w_out, norm_pre_mix, norm_post_mix,
              norm_pre_ffn, norm_post_ffn, w_ffn_gate, w_ffn_up, w_ffn_down):
    lb_sm = jax.nn.softmax(hgrn_lb_logits.astype(jnp.float32), axis=0)
    lb_all = jnp.cumsum(lb_sm, axis=0) - lb_sm[0:1]
    y_prompt = x_prompt
    y_sample = x_sample
    for l in range(DEPTH):
        layer_params = (w_in[l], conv_w[l], gdn_a_log[l], gdn_dt_bias[l], gdn_norm_w[l], lb_all[l],
                        hgrn_norm_w[l], w_branch_gdn[l], w_branch_hgrn[l], w_out[l], norm_pre_mix[l],
                        norm_post_mix[l], norm_pre_ffn[l], norm_post_ffn[l], w_ffn_gate[l], w_ffn_up[l],
                        w_ffn_down[l])
        y_prompt = hybrid_layer(y_prompt, *layer_params)
        y_sample = hybrid_layer(y_sample, *layer_params)
    return (y_prompt, y_sample)
```

```python
import functools

import jax
import jax.numpy as jnp
from jax import lax
from jax.experimental import pallas as pl
from jax.experimental.pallas import tpu as pltpu

F32 = jnp.float32
BF16 = jnp.bfloat16

D_MODEL = 1024
N_HEADS = 4
D_HEAD = 128
HEADS_W = N_HEADS * D_HEAD
GDN_CONV = 5
GDN_CHUNK = 64
HG_CHUNK = 16
NORM_EPS = 1e-6
HALO = 16
LANES = 128
SEQ_BLOCK = 256
VMEM_LIMIT = 56 * 1024 * 1024


def _dot(a, b):
    return jnp.dot(a, b, preferred_element_type=F32)


def _dot_nt(a, b):
    return lax.dot_general(a, b, (((1,), (1,)), ((), ())), preferred_element_type=F32)


def _dot_tn(a, b):
    return lax.dot_general(a, b, (((0,), (0,)), ((), ())), preferred_element_type=F32)


def _sigmoid(x):
    return 1.0 / (1.0 + jnp.exp(-x))


def _silu(x):
    return x * _sigmoid(x)


def _softplus(x):
    return jnp.maximum(x, 0.0) + jnp.log(1.0 + jnp.exp(-jnp.abs(x)))


def _seg_scan(x, group, axis, reverse):
    n = x.shape[axis]
    pos = lax.broadcasted_iota(jnp.int32, x.shape, axis) & (group - 1)
    s = 1
    while s < group:
        if reverse:
            x = x + jnp.where(pos < group - s, pltpu.roll(x, n - s, axis), 0.0)
        else:
            x = x + jnp.where(pos >= s, pltpu.roll(x, s, axis), 0.0)
        s *= 2
    return x


def _const_spec(shape):
    return pl.BlockSpec(shape, lambda *_: (0,) * len(shape), pipeline_mode=pl.Buffered(1))


def _in_proj_kernel(x_ref, xp_ref, xn_ref, npre_ref, wqkv_ref, wz_ref, wab_ref, wabt_ref,
                    whq_ref, whf_ref, whi_ref, whg_ref, wgate_ref, convw_ref, gpc_ref, gpr_ref,
                    lb_ref,
                    qkv_ref, zs_ref, gcol_ref, grow_ref, hq_ref, hlf_ref, hi_ref, hgs_ref,
                    gate_ref, ext_ref):
    i = pl.program_id(1)
    nt = pl.num_programs(1)
    tm = x_ref.shape[1]
    nw = npre_ref[...]

    def norm(xv):
        ms = jnp.mean(xv * xv, axis=-1, keepdims=True)
        return (xv * lax.rsqrt(ms + NORM_EPS) * nw).astype(BF16)

    h = norm(x_ref[0])

    ext_ref[HALO:HALO + tm, :] = _dot(h, wqkv_ref[...])
    has_prev = (i > 0).astype(F32)
    has_next = (i < nt - 1).astype(F32)
    ext_ref[0:HALO, :] = _dot(norm(xp_ref[0]), wqkv_ref[...]) * has_prev
    ext_ref[HALO + tm:, :] = _dot(norm(xn_ref[0]), wqkv_ref[...]) * has_next
    for g in range(3 * N_HEADS):
        cs = slice(g * D_HEAD, (g + 1) * D_HEAD)
        acc = convw_ref[0:1, cs] * ext_ref[HALO - 2:HALO - 2 + tm, cs]
        for j in range(1, GDN_CONV):
            acc = acc + convw_ref[j:j + 1, cs] * ext_ref[HALO - 2 + j:HALO - 2 + j + tm, cs]
        y = _silu(acc)
        if g < 2 * N_HEADS:
            y = y * lax.rsqrt(jnp.sum(y * y, axis=-1, keepdims=True) + NORM_EPS)
        if g < N_HEADS:
            y = y * (D_HEAD ** -0.5)
        qkv_ref[0, :, cs] = y.astype(BF16)

    zs_ref[0] = _silu(_dot(h, wz_ref[...])).astype(BF16)

    ab = _dot(h, wab_ref[...])
    lane = lax.broadcasted_iota(jnp.int32, ab.shape, 1)
    gdec = -jnp.exp(gpc_ref[0:1, :]) * _softplus(ab + gpc_ref[1:2, :])
    cum = jnp.where(lane < N_HEADS, _seg_scan(gdec, GDN_CHUNK, 0, False),
                    _seg_scan(gdec, GDN_CHUNK, 0, True))
    gcol = jnp.where(lane < 2 * N_HEADS, cum, _sigmoid(ab))
    gcol_ref[0] = gcol[:, 0:4 * N_HEADS]
    abt = _dot_nt(wabt_ref[...], h)
    gdec_t = -jnp.exp(gpr_ref[:, 0:1]) * _softplus(abt + gpr_ref[:, 1:2])
    sub = lax.broadcasted_iota(jnp.int32, abt.shape, 0)
    cum_t = jnp.where(sub < N_HEADS, _seg_scan(gdec_t, GDN_CHUNK, 1, False),
                      _seg_scan(gdec_t, GDN_CHUNK, 1, True))
    grow_ref[0] = cum_t[0:2 * N_HEADS, :]

    hq_ref[0] = (_silu(_dot(h, whq_ref[...])) * (D_HEAD ** -0.5)).astype(BF16)
    lb = lb_ref[...]
    hlf_ref[0] = jnp.log(lb + (1.0 - lb) * _sigmoid(_dot(h, whf_ref[...])))
    hi_ref[0] = _dot(h, whi_ref[...]).astype(BF16)
    hgs_ref[0] = _silu(_dot(h, whg_ref[...])).astype(BF16)

    gate_ref[0] = _sigmoid(_dot(h, wgate_ref[...])).astype(BF16)


def _in_proj(x, p, tm):
    b, l, d = x.shape
    nt = l // tm
    hb = tm // HALO
    nhb = l // HALO
    grid = (b, nt)
    row = lambda w: pl.BlockSpec((1, tm, w), lambda bi, i: (bi, i, 0))
    in_specs = [
        row(d),
        pl.BlockSpec((1, HALO, d), lambda bi, i: (bi, jnp.maximum(i * hb - 1, 0), 0)),
        pl.BlockSpec((1, HALO, d), lambda bi, i: (bi, jnp.minimum((i + 1) * hb, nhb - 1), 0)),
    ] + [_const_spec(p[k].shape) for k in
         ('n_pre_mix', 'w_qkv', 'w_z', 'w_ab', 'w_abt', 'w_hq', 'w_hf', 'w_hi', 'w_hg', 'w_gate',
          'conv_w', 'gpar_c', 'gpar_r', 'lb')]
    out_shape = [
        jax.ShapeDtypeStruct((b, l, 3 * HEADS_W), BF16),
        jax.ShapeDtypeStruct((b, l, HEADS_W), BF16),
        jax.ShapeDtypeStruct((b, l, 4 * N_HEADS), F32),
        jax.ShapeDtypeStruct((b, 2 * N_HEADS, l), F32),
        jax.ShapeDtypeStruct((b, l, HEADS_W), BF16),
        jax.ShapeDtypeStruct((b, l, 2 * HEADS_W), F32),
        jax.ShapeDtypeStruct((b, l, HEADS_W), BF16),
        jax.ShapeDtypeStruct((b, l, HEADS_W), BF16),
        jax.ShapeDtypeStruct((b, l, 2 * D_MODEL), BF16),
    ]
    out_specs = [row(3 * HEADS_W), row(HEADS_W), row(4 * N_HEADS),
                 pl.BlockSpec((1, 2 * N_HEADS, tm), lambda bi, i: (bi, 0, i)),
                 row(HEADS_W), row(2 * HEADS_W), row(HEADS_W), row(HEADS_W), row(2 * D_MODEL)]
    return pl.pallas_call(
        _in_proj_kernel,
        grid=grid,
        in_specs=in_specs,
        out_specs=out_specs,
        out_shape=out_shape,
        scratch_shapes=[pltpu.VMEM((tm + 2 * HALO, 3 * HEADS_W), F32)],
        compiler_params=pltpu.CompilerParams(
            dimension_semantics=("parallel", "parallel"), vmem_limit_bytes=VMEM_LIMIT),
        name="in_proj",
    )(x, x, x, p['n_pre_mix'], p['w_qkv'], p['w_z'], p['w_ab'], p['w_abt'], p['w_hq'], p['w_hf'],
      p['w_hi'], p['w_hg'], p['w_gate'], p['conv_w'], p['gpar_c'], p['gpar_r'], p['lb'])


def _gdn_head(d, h, q_ref, k_ref, v_ref, gc_ref, gr_ref, o_ref, s_ref):
    hs = slice(h * D_HEAD, (h + 1) * D_HEAD)
    q = q_ref[0, :, hs]
    k = k_ref[0, :, hs]
    v = v_ref[0, :, hs]
    t = q.shape[0]
    nc = t // GDN_CHUNK
    li = d * N_HEADS + h
    cumc = gc_ref[0, :, li:li + 1]
    beta = gc_ref[0, :, 2 * N_HEADS + li:2 * N_HEADS + li + 1]
    cumr = gr_ref[0, li:li + 1, :]

    row = lax.broadcasted_iota(jnp.int32, (t, t), 0)
    col = lax.broadcasted_iota(jnp.int32, (t, t), 1)
    same = (row ^ col) < GDN_CHUNK
    if d == 0:
        incl = same & (col <= row)
        strict = same & (col < row)
    else:
        incl = same & (col >= row)
        strict = same & (col > row)
    dec = jnp.where(incl, jnp.exp(jnp.where(incl, cumc - cumr, 0.0)), 0.0)

    a = jnp.where(strict, _dot_nt(k, k) * dec, 0.0) * beta
    qkm = (_dot_nt(q, k) * dec).astype(BF16)

    tinv = (row == col).astype(F32) - a
    pw = a.astype(BF16)
    s = 2
    while s < GDN_CHUNK:
        pw = _dot(pw, pw).astype(BF16)
        tinv = tinv + _dot(tinv.astype(BF16), pw)
        s *= 2
    tinv = tinv.astype(BF16)

    e = jnp.exp(cumc)
    kf = k.astype(F32)
    u = _dot(tinv, (v.astype(F32) * beta).astype(BF16))
    w = _dot(tinv, (kf * (beta * e)).astype(BF16)).astype(BF16)
    qdec = (q.astype(F32) * e).astype(BF16)

    st = s_ref[li]
    oq = [None] * nc
    vn = [None] * nc
    for c in (range(nc) if d == 0 else reversed(range(nc))):
        rs = slice(c * GDN_CHUNK, (c + 1) * GDN_CHUNK)
        last_row = (c + 1) * GDN_CHUNK - 1 if d == 0 else c * GDN_CHUNK
        tot = cumc[last_row:last_row + 1, :]
        m1 = _dot(jnp.concatenate([w[rs], qdec[rs]], axis=0), st.astype(BF16))
        vnew = (u[rs] - m1[:GDN_CHUNK]).astype(BF16)
        oq[c] = m1[GDN_CHUNK:]
        vn[c] = vnew
        kdec = (kf[rs] * jnp.exp(tot - cumc[rs])).astype(BF16)
        st = st * jnp.exp(tot) + _dot_tn(kdec, vnew)
    s_ref[li] = st
    o = jnp.concatenate(oq, axis=0) + _dot(qkm, jnp.concatenate(vn, axis=0))
    o_ref[0, :, hs] = o.astype(o_ref.dtype)


def _gdn_kernel(qf, kf, vf, gcf, grf, qb, kb, vb, gcb, grb, of_ref, ob_ref, s_ref):
    @pl.when(pl.program_id(1) == 0)
    def _():
        s_ref[...] = jnp.zeros_like(s_ref)

    for h in range(N_HEADS):
        _gdn_head(0, h, qf, kf, vf, gcf, grf, of_ref, s_ref)
        _gdn_head(1, h, qb, kb, vb, gcb, grb, ob_ref, s_ref)


def _gdn(qkv, gcol, grow):
    b, l, _ = qkv.shape
    t = SEQ_BLOCK
    nb = l // t
    fwd = lambda c: pl.BlockSpec((1, t, HEADS_W), lambda bi, i: (bi, i, c))
    bwd = lambda c: pl.BlockSpec((1, t, HEADS_W), lambda bi, i: (bi, nb - 1 - i, c))
    in_specs = [
        fwd(0), fwd(1), fwd(2),
        pl.BlockSpec((1, t, 4 * N_HEADS), lambda bi, i: (bi, i, 0)),
        pl.BlockSpec((1, 2 * N_HEADS, t), lambda bi, i: (bi, 0, i)),
        bwd(0), bwd(1), bwd(2),
        pl.BlockSpec((1, t, 4 * N_HEADS), lambda bi, i: (bi, nb - 1 - i, 0)),
        pl.BlockSpec((1, 2 * N_HEADS, t), lambda bi, i: (bi, 0, nb - 1 - i)),
    ]
    out = jax.ShapeDtypeStruct((b, l, HEADS_W), BF16)
    return pl.pallas_call(
        _gdn_kernel,
        grid=(b, nb),
        in_specs=in_specs,
        out_specs=[fwd(0), bwd(0)],
        out_shape=[out, out],
        scratch_shapes=[pltpu.VMEM((2 * N_HEADS, D_HEAD, D_HEAD), F32)],
        compiler_params=pltpu.CompilerParams(
            dimension_semantics=("parallel", "arbitrary"), vmem_limit_bytes=VMEM_LIMIT),
        name="gdn",
    )(qkv, qkv, qkv, gcol, grow, qkv, qkv, qkv, gcol, grow)


def _hgrn_head(d, h, q_ref, lf_ref, v_ref, o_ref, s_ref):
    hs = slice(h * D_HEAD, (h + 1) * D_HEAD)
    ls = slice(d * HEADS_W + h * D_HEAD, d * HEADS_W + (h + 1) * D_HEAD)
    q = q_ref[0, :, hs].astype(F32)
    v = v_ref[0, :, hs]
    lf = lf_ref[0, :, ls]
    t = q.shape[0]
    nc = t // HG_CHUNK
    li = d * N_HEADS + h

    pre = _seg_scan(lf, HG_CHUNK, 0, False)
    suf = _seg_scan(lf, HG_CHUNK, 0, True)
    cum, rest = (pre, suf - lf) if d == 0 else (suf, pre - lf)
    kk = 1.0 - jnp.exp(lf)
    qdec = (q * jnp.exp(cum)).astype(BF16)
    kinv = (kk * jnp.exp(-cum)).astype(BF16)
    kdec = (kk * jnp.exp(rest)).astype(BF16)
    last = jnp.exp(cum + rest)

    row = lax.broadcasted_iota(jnp.int32, (t, t), 0)
    col = lax.broadcasted_iota(jnp.int32, (t, t), 1)
    same = (row ^ col) < HG_CHUNK
    mask = same & ((col <= row) if d == 0 else (col >= row))
    attn = jnp.where(mask, _dot_nt(qdec, kinv), 0.0).astype(BF16)
    o_intra = _dot(attn, v)

    st = s_ref[li]
    oi = [None] * nc
    for c in (range(nc) if d == 0 else reversed(range(nc))):
        rs = slice(c * HG_CHUNK, (c + 1) * HG_CHUNK)
        oi[c] = _dot_nt(qdec[rs], st.astype(BF16))
        st = st * last[c * HG_CHUNK:c * HG_CHUNK + 1, :] + _dot_tn(v[rs], kdec[rs])
    s_ref[li] = st
    o_ref[0, :, hs] = (o_intra + jnp.concatenate(oi, axis=0)).astype(o_ref.dtype)


def _hgrn_kernel(qf, lff, vf, qb, lfb, vb, of_ref, ob_ref, s_ref):
    @pl.when(pl.program_id(1) == 0)
    def _():
        s_ref[...] = jnp.zeros_like(s_ref)

    for h in range(N_HEADS):
        _hgrn_head(0, h, qf, lff, vf, of_ref, s_ref)
        _hgrn_head(1, h, qb, lfb, vb, ob_ref, s_ref)


def _hgrn(hq, hlf, hi):
    b, l, _ = hq.shape
    t = SEQ_BLOCK
    nb = l // t
    fwd = lambda w: pl.BlockSpec((1, t, w), lambda bi, i: (bi, i, 0))
    bwd = lambda w: pl.BlockSpec((1, t, w), lambda bi, i: (bi, nb - 1 - i, 0))
    out = jax.ShapeDtypeStruct((b, l, HEADS_W), BF16)
    return pl.pallas_call(
        _hgrn_kernel,
        grid=(b, nb),
        in_specs=[fwd(HEADS_W), fwd(2 * HEADS_W), fwd(HEADS_W),
                  bwd(HEADS_W), bwd(2 * HEADS_W), bwd(HEADS_W)],
        out_specs=[fwd(HEADS_W), bwd(HEADS_W)],
        out_shape=[out, out],
        scratch_shapes=[pltpu.VMEM((2 * N_HEADS, D_HEAD, D_HEAD), F32)],
        compiler_params=pltpu.CompilerParams(
            dimension_semantics=("parallel", "arbitrary"), vmem_limit_bytes=VMEM_LIMIT),
        name="hgrn",
    )(hq, hlf, hi, hq, hlf, hi)


def _head_norm_gate(of_ref, ob_ref, nw_ref, gs_ref):
    parts = []
    for h in range(N_HEADS):
        hs = slice(h * D_HEAD, (h + 1) * D_HEAD)
        o = of_ref[:, hs].astype(F32) + ob_ref[:, hs].astype(F32)
        o = o * lax.rsqrt(jnp.mean(o * o, axis=-1, keepdims=True) + NORM_EPS) * nw_ref[...]
        parts.append((o * gs_ref[:, hs].astype(F32)).astype(BF16))
    return jnp.concatenate(parts, axis=1)


def _mix_out_kernel(x_ref, af_ref, ab_ref, zs_ref, bf_ref, bb_ref, gs_ref, gate_ref,
                    gnw_ref, hnw_ref, wa_ref, wb_ref, wo_ref, npost_ref, y_ref):
    oa = _head_norm_gate(af_ref, ab_ref, gnw_ref, zs_ref)
    ob = _head_norm_gate(bf_ref, bb_ref, hnw_ref, gs_ref)
    ya = _dot(oa, wa_ref[...])
    yb = _dot(ob, wb_ref[...])
    merged = (gate_ref[:, 0:D_MODEL].astype(F32) * ya
              + gate_ref[:, D_MODEL:].astype(F32) * yb).astype(BF16)
    r = _dot(merged, wo_ref[...])
    r = r * lax.rsqrt(jnp.mean(r * r, axis=-1, keepdims=True) + NORM_EPS) * npost_ref[...]
    y_ref[...] = x_ref[...] + r


def _mix_out(x2, af, ab, zs, bf, bb, gs, gate, p, tm):
    m, d = x2.shape
    row = lambda w: pl.BlockSpec((tm, w), lambda i: (i, 0))
    return pl.pallas_call(
        _mix_out_kernel,
        grid=(m // tm,),
        in_specs=[row(d)] + [row(HEADS_W)] * 6 + [row(2 * d)] + [
            _const_spec(p[k].shape) for k in
            ('gdn_norm_w', 'hgrn_norm_w', 'w_br_gdn', 'w_br_hg', 'w_out', 'n_post_mix')],
        out_specs=row(d),
        out_shape=jax.ShapeDtypeStruct((m, d), F32),
        compiler_params=pltpu.CompilerParams(
            dimension_semantics=("parallel",), vmem_limit_bytes=VMEM_LIMIT),
        name="mix_out",
    )(x2, af, ab, zs, bf, bb, gs, gate, p['gdn_norm_w'], p['hgrn_norm_w'], p['w_br_gdn'],
      p['w_br_hg'], p['w_out'], p['n_post_mix'])


def _ffn_kernel(x_ref, npre_ref, wg_ref, wu_ref, wd_ref, npost_ref, y_ref):
    x = x_ref[...]
    h = (x * lax.rsqrt(jnp.mean(x * x, axis=-1, keepdims=True) + NORM_EPS)
         * npre_ref[...]).astype(BF16)
    act = (_silu(_dot(h, wg_ref[...])) * _dot(h, wu_ref[...])).astype(BF16)
    ff = _dot(act, wd_ref[...])
    ff = ff * lax.rsqrt(jnp.mean(ff * ff, axis=-1, keepdims=True) + NORM_EPS) * npost_ref[...]
    y_ref[...] = x + ff


def _ffn(x2, p, tm):
    m, d = x2.shape
    row = pl.BlockSpec((tm, d), lambda i: (i, 0))
    return pl.pallas_call(
        _ffn_kernel,
        grid=(m // tm,),
        in_specs=[row] + [_const_spec(p[k].shape) for k in
                          ('n_pre_ffn', 'w_ffn_gate', 'w_ffn_up', 'w_ffn_down', 'n_post_ffn')],
        out_specs=row,
        out_shape=jax.ShapeDtypeStruct((m, d), F32),
        compiler_params=pltpu.CompilerParams(
            dimension_semantics=("parallel",), vmem_limit_bytes=VMEM_LIMIT),
        name="ffn",
    )(x2, p['n_pre_ffn'], p['w_ffn_gate'], p['w_ffn_up'], p['w_ffn_down'], p['n_post_ffn'])


def _layer_params(l, w_in, conv_w, a_log, dt_bias, gdn_norm_w, lb_all, hgrn_norm_w, w_br_gdn,
                  w_br_hg, w_out, n_pre_mix, n_post_mix, n_pre_ffn, n_post_ffn, w_gate, w_up,
                  w_down):
    wi = w_in[l]
    o = 0
    parts = {}
    for name, width in (('w_qkv', 3 * HEADS_W), ('w_z', HEADS_W), ('ab', 4 * N_HEADS),
                        ('w_hq', HEADS_W), ('w_hf', 2 * HEADS_W), ('w_hi', HEADS_W),
                        ('w_hg', HEADS_W), ('w_gate', 2 * D_MODEL)):
        parts[name] = wi[:, o:o + width].astype(BF16)
        o += width
    ab = parts.pop('ab')
    parts['w_ab'] = jnp.pad(ab, ((0, 0), (0, LANES - 4 * N_HEADS)))
    parts['w_abt'] = ab.T
    gpar = jnp.stack([a_log[l].reshape(-1), dt_bias[l].reshape(-1)]).astype(F32)
    parts['gpar_c'] = jnp.pad(gpar, ((0, 0), (0, LANES - 2 * N_HEADS)))
    parts['gpar_r'] = jnp.pad(gpar.T, ((0, 2 * N_HEADS), (0, 0)))
    vec = lambda a: a.astype(F32).reshape(1, -1)
    parts.update(
        conv_w=conv_w[l].astype(F32), lb=vec(lb_all[l]),
        gdn_norm_w=vec(gdn_norm_w[l]), hgrn_norm_w=vec(hgrn_norm_w[l]),
        w_br_gdn=w_br_gdn[l].astype(BF16), w_br_hg=w_br_hg[l].astype(BF16),
        w_out=w_out[l].astype(BF16),
        n_pre_mix=vec(n_pre_mix[l]), n_post_mix=vec(n_post_mix[l]),
        n_pre_ffn=vec(n_pre_ffn[l]), n_post_ffn=vec(n_post_ffn[l]),
        w_ffn_gate=w_gate[l].astype(BF16), w_ffn_up=w_up[l].astype(BF16),
        w_ffn_down=w_down[l].astype(BF16))
    return parts


def _layer(x, p, tm_proj=512, tm_out=256):
    b, l, d = x.shape
    qkv, zs, gcol, grow, hq, hlf, hi, hgs, gate = _in_proj(x, p, min(tm_proj, l))
    af, ab = _gdn(qkv, gcol, grow)
    bf, bb = _hgrn(hq, hlf, hi)
    flat = lambda a: a.reshape(b * l, a.shape[-1])
    x2 = _mix_out(flat(x), flat(af), flat(ab), flat(zs), flat(bf), flat(bb), flat(hgs),
                  flat(gate), p, tm_out)
    return _ffn(x2, p, tm_out).reshape(b, l, d)


def kernel(x_prompt, x_sample, w_in, conv_w, gdn_a_log, gdn_dt_bias, gdn_norm_w, hgrn_lb_logits,
           hgrn_norm_w, w_branch_gdn, w_branch_hgrn, w_out, norm_pre_mix, norm_post_mix,
           norm_pre_ffn, norm_post_ffn, w_ffn_gate, w_ffn_up, w_ffn_down):
    lb_sm = jax.nn.softmax(hgrn_lb_logits.astype(F32), axis=0)
    lb_all = jnp.cumsum(lb_sm, axis=0) - lb_sm[0:1]
    y_prompt, y_sample = x_prompt, x_sample
    for l in range(w_in.shape[0]):
        p = _layer_params(l, w_in, conv_w, gdn_a_log, gdn_dt_bias, gdn_norm_w, lb_all,
                          hgrn_norm_w, w_branch_gdn, w_branch_hgrn, w_out, norm_pre_mix,
                          norm_post_mix, norm_pre_ffn, norm_post_ffn, w_ffn_gate, w_ffn_up,
                          w_ffn_down)
        y_prompt = _layer(y_prompt, p)
        y_sample = _layer(y_sample, p)
    return (y_prompt, y_sample)
```

```python
import functools

import jax
import jax.numpy as jnp
from jax import lax
from jax.experimental import pallas as pl
from jax.experimental.pallas import tpu as pltpu

F32 = jnp.float32
BF16 = jnp.bfloat16

D_MODEL = 1024
N_HEADS = 4
D_HEAD = 128
HEADS_W = N_HEADS * D_HEAD
GDN_CONV = 5
GDN_CHUNK = 64
HG_CHUNK = 16
NORM_EPS = 1e-6
HALO = 16
LANES = 128
SEQ_BLOCK = 256
VMEM_LIMIT = 56 * 1024 * 1024


def _dot(a, b):
    return jnp.dot(a, b, preferred_element_type=F32)


def _dot_nt(a, b):
    return lax.dot_general(a, b, (((1,), (1,)), ((), ())), preferred_element_type=F32)


def _dot_tn(a, b):
    return lax.dot_general(a, b, (((0,), (0,)), ((), ())), preferred_element_type=F32)


def _sigmoid(x):
    return 1.0 / (1.0 + jnp.exp(-x))


def _sigmoid_t(x):
    return 0.5 * jnp.tanh(0.5 * x) + 0.5


def _silu(x):
    return x * _sigmoid_t(x)


def _softplus(x):
    return jnp.maximum(x, 0.0) + jnp.log(1.0 + jnp.exp(-jnp.abs(x)))


def _seg_scan(x, group, axis, reverse):
    n = x.shape[axis]
    pos = lax.broadcasted_iota(jnp.int32, x.shape, axis) & (group - 1)
    s = 1
    while s < group:
        if reverse:
            x = x + jnp.where(pos < group - s, pltpu.roll(x, n - s, axis), 0.0)
        else:
            x = x + jnp.where(pos >= s, pltpu.roll(x, s, axis), 0.0)
        s *= 2
    return x


def _const_spec(shape):
    return pl.BlockSpec(shape, lambda *_: (0,) * len(shape), pipeline_mode=pl.Buffered(1))


def _in_proj_kernel(x_ref, xp_ref, xn_ref, npre_ref, wqkv_ref, wz_ref, wab_ref, wabt_ref,
                    whq_ref, whf_ref, whi_ref, whg_ref, wgate_ref, convw_ref, gpc_ref, gpr_ref,
                    lb_ref,
                    qkv_ref, zs_ref, gcol_ref, grow_ref, hq_ref, hlf_ref, hi_ref, hgs_ref,
                    gate_ref, ext_ref):
    i = pl.program_id(1)
    nt = pl.num_programs(1)
    tm = x_ref.shape[1]
    nw = npre_ref[...]

    def norm(xv):
        ms = jnp.mean(xv * xv, axis=-1, keepdims=True)
        return (xv * lax.rsqrt(ms + NORM_EPS) * nw).astype(BF16)

    h = norm(x_ref[0])

    ext_ref[HALO:HALO + tm, :] = _dot(h, wqkv_ref[...])
    has_prev = (i > 0).astype(F32)
    has_next = (i < nt - 1).astype(F32)
    ext_ref[0:HALO, :] = _dot(norm(xp_ref[0]), wqkv_ref[...]) * has_prev
    ext_ref[HALO + tm:, :] = _dot(norm(xn_ref[0]), wqkv_ref[...]) * has_next
    for g in range(3 * N_HEADS):
        cs = slice(g * D_HEAD, (g + 1) * D_HEAD)
        ext = ext_ref[:, cs]
        half = GDN_CONV // 2
        acc = convw_ref[half:half + 1, cs] * ext[HALO:HALO + tm]
        for j in range(GDN_CONV):
            if j != half:
                shifted = pltpu.roll(ext, (half - j) % (tm + 2 * HALO), 0)
                acc = acc + convw_ref[j:j + 1, cs] * shifted[HALO:HALO + tm]
        y = _silu(acc)
        if g < 2 * N_HEADS:
            y = y * lax.rsqrt(jnp.sum(y * y, axis=-1, keepdims=True) + NORM_EPS)
        if g < N_HEADS:
            y = y * (D_HEAD ** -0.5)
        qkv_ref[0, :, cs] = y.astype(BF16)

    zs_ref[0] = _silu(_dot(h, wz_ref[...])).astype(BF16)

    ab = _dot(h, wab_ref[...])
    lane = lax.broadcasted_iota(jnp.int32, ab.shape, 1)
    gdec = -jnp.exp(gpc_ref[0:1, :]) * _softplus(ab + gpc_ref[1:2, :])
    cum = jnp.where(lane < N_HEADS, _seg_scan(gdec, GDN_CHUNK, 0, False),
                    _seg_scan(gdec, GDN_CHUNK, 0, True))
    gcol = jnp.where(lane < 2 * N_HEADS, cum, _sigmoid_t(ab))
    gcol_ref[0] = gcol[:, 0:4 * N_HEADS]
    abt = _dot_nt(wabt_ref[...], h)
    gdec_t = -jnp.exp(gpr_ref[:, 0:1]) * _softplus(abt + gpr_ref[:, 1:2])
    sub = lax.broadcasted_iota(jnp.int32, abt.shape, 0)
    cum_t = jnp.where(sub < N_HEADS, _seg_scan(gdec_t, GDN_CHUNK, 1, False),
                      _seg_scan(gdec_t, GDN_CHUNK, 1, True))
    grow_ref[0] = cum_t[0:2 * N_HEADS, :]

    hq_ref[0] = (_silu(_dot(h, whq_ref[...])) * (D_HEAD ** -0.5)).astype(BF16)
    lb = lb_ref[...]
    hlf_ref[0] = jnp.log(lb + (1.0 - lb) * _sigmoid(_dot(h, whf_ref[...])))
    hi_ref[0] = _dot(h, whi_ref[...]).astype(BF16)
    hgs_ref[0] = _silu(_dot(h, whg_ref[...])).astype(BF16)

    gate_ref[0] = _sigmoid_t(_dot(h, wgate_ref[...])).astype(BF16)


def _in_proj(x, p, tm):
    b, l, d = x.shape
    nt = l // tm
    hb = tm // HALO
    nhb = l // HALO
    grid = (b, nt)
    row = lambda w: pl.BlockSpec((1, tm, w), lambda bi, i: (bi, i, 0))
    in_specs = [
        row(d),
        pl.BlockSpec((1, HALO, d), lambda bi, i: (bi, jnp.maximum(i * hb - 1, 0), 0)),
        pl.BlockSpec((1, HALO, d), lambda bi, i: (bi, jnp.minimum((i + 1) * hb, nhb - 1), 0)),
    ] + [_const_spec(p[k].shape) for k in
         ('n_pre_mix', 'w_qkv', 'w_z', 'w_ab', 'w_abt', 'w_hq', 'w_hf', 'w_hi', 'w_hg', 'w_gate',
          'conv_w', 'gpar_c', 'gpar_r', 'lb')]
    out_shape = [
        jax.ShapeDtypeStruct((b, l, 3 * HEADS_W), BF16),
        jax.ShapeDtypeStruct((b, l, HEADS_W), BF16),
        jax.ShapeDtypeStruct((b, l, 4 * N_HEADS), F32),
        jax.ShapeDtypeStruct((b, 2 * N_HEADS, l), F32),
        jax.ShapeDtypeStruct((b, l, HEADS_W), BF16),
        jax.ShapeDtypeStruct((b, l, 2 * HEADS_W), F32),
        jax.ShapeDtypeStruct((b, l, HEADS_W), BF16),
        jax.ShapeDtypeStruct((b, l, HEADS_W), BF16),
        jax.ShapeDtypeStruct((b, l, 2 * D_MODEL), BF16),
    ]
    out_specs = [row(3 * HEADS_W), row(HEADS_W), row(4 * N_HEADS),
                 pl.BlockSpec((1, 2 * N_HEADS, tm), lambda bi, i: (bi, 0, i)),
                 row(HEADS_W), row(2 * HEADS_W), row(HEADS_W), row(HEADS_W), row(2 * D_MODEL)]
    return pl.pallas_call(
        _in_proj_kernel,
        grid=grid,
        in_specs=in_specs,
        out_specs=out_specs,
        out_shape=out_shape,
        scratch_shapes=[pltpu.VMEM((tm + 2 * HALO, 3 * HEADS_W), F32)],
        compiler_params=pltpu.CompilerParams(
            dimension_semantics=("parallel", "parallel"), vmem_limit_bytes=VMEM_LIMIT),
        name="in_proj",
    )(x, x, x, p['n_pre_mix'], p['w_qkv'], p['w_z'], p['w_ab'], p['w_abt'], p['w_hq'], p['w_hf'],
      p['w_hi'], p['w_hg'], p['w_gate'], p['conv_w'], p['gpar_c'], p['gpar_r'], p['lb'])


def _gdn_kernel(qf, kf, vf, gcf, grf, qb, kb, vb, gcb, grb, of_ref, ob_ref, s_ref):
    @pl.when(pl.program_id(1) == 0)
    def _():
        s_ref[...] = jnp.zeros_like(s_ref)

    t = qf.shape[1]
    nc = t // GDN_CHUNK
    refs = ((qf, kf, vf, gcf, grf, of_ref), (qb, kb, vb, gcb, grb, ob_ref))
    streams = [(d, h) for h in range(N_HEADS) for d in (0, 1)]

    same = ((lax.broadcasted_iota(jnp.int32, (t, t), 0) ^ lax.broadcasted_iota(jnp.int32, (t, t), 1))
            < GDN_CHUNK)
    wrow = lax.broadcasted_iota(jnp.int32, (GDN_CHUNK, t), 0)
    wlane = lax.broadcasted_iota(jnp.int32, (GDN_CHUNK, t), 1)
    wcol = wlane & (GDN_CHUNK - 1)
    incl = (wcol <= wrow, wcol >= wrow)
    offdiag = wcol != wrow
    eye = (wcol == wrow).astype(F32)

    def to_wide(tall):
        tall = jnp.where(same, tall, 0.0)
        out = tall[0:GDN_CHUNK]
        for c in range(1, nc):
            out = out + tall[c * GDN_CHUNK:(c + 1) * GDN_CHUNK]
        return out

    def to_tall(wide):
        return jnp.where(same, jnp.concatenate([wide] * nc, axis=0), 0.0)

    def col_to_wide(colv):
        out = jnp.broadcast_to(colv[0:GDN_CHUNK], (GDN_CHUNK, t))
        for c in range(1, nc):
            out = jnp.where(wlane >= c * GDN_CHUNK, colv[c * GDN_CHUNK:(c + 1) * GDN_CHUNK], out)
        return out

    q, k, v, cumc, beta, a, qkm = {}, {}, {}, {}, {}, {}, {}
    for s in streams:
        d, h = s
        q_ref, k_ref, v_ref, gc_ref, gr_ref, _ = refs[d]
        hs = slice(h * D_HEAD, (h + 1) * D_HEAD)
        li = d * N_HEADS + h
        q[s] = q_ref[0, :, hs]
        k[s] = k_ref[0, :, hs]
        v[s] = v_ref[0, :, hs]
        cumc[s] = gc_ref[0, :, li:li + 1]
        beta[s] = gc_ref[0, :, 2 * N_HEADS + li:2 * N_HEADS + li + 1]
        cumr = gr_ref[0, li:li + 1, :]
        dec = jnp.where(incl[d], jnp.exp(jnp.where(incl[d], col_to_wide(cumc[s]) - cumr, 0.0)),
                        0.0)
        a[s] = (jnp.where(offdiag, to_wide(_dot_nt(k[s], k[s])) * dec, 0.0)
                * col_to_wide(beta[s]))
        qkm[s] = to_tall((to_wide(_dot_nt(q[s], k[s])) * dec).astype(BF16))

    tinv = {s: eye - a[s] for s in streams}
    pw = {s: a[s].astype(BF16) for s in streams}
    span = 2
    while span < GDN_CHUNK:
        for s in streams:
            pw[s] = _dot(pw[s], to_tall(pw[s])).astype(BF16)
        for s in streams:
            tinv[s] = tinv[s] + _dot(tinv[s].astype(BF16), to_tall(pw[s]))
        span *= 2

    u, w, qdec, kf32 = {}, {}, {}, {}
    for s in streams:
        e = jnp.exp(cumc[s])
        kf32[s] = k[s].astype(F32)
        rhs = jnp.concatenate([(v[s].astype(F32) * beta[s]).astype(BF16),
                               (kf32[s] * (beta[s] * e)).astype(BF16)], axis=1)
        uw = _dot(to_tall(tinv[s].astype(BF16)), rhs)
        u[s] = uw[:, :D_HEAD]
        w[s] = uw[:, D_HEAD:].astype(BF16)
        qdec[s] = (q[s].astype(F32) * e).astype(BF16)

    st = {s: s_ref[s[0] * N_HEADS + s[1]] for s in streams}
    oq = {s: [None] * nc for s in streams}
    vn = {s: [None] * nc for s in streams}
    for j in range(nc):
        m1 = {}
        for s in streams:
            c = j if s[0] == 0 else nc - 1 - j
            rs = slice(c * GDN_CHUNK, (c + 1) * GDN_CHUNK)
            m1[s] = _dot(jnp.concatenate([w[s][rs], qdec[s][rs]], axis=0), st[s].astype(BF16))
        for s in streams:
            c = j if s[0] == 0 else nc - 1 - j
            rs = slice(c * GDN_CHUNK, (c + 1) * GDN_CHUNK)
            last_row = (c + 1) * GDN_CHUNK - 1 if s[0] == 0 else c * GDN_CHUNK
            tot = cumc[s][last_row:last_row + 1, :]
            vnew = (u[s][rs] - m1[s][:GDN_CHUNK]).astype(BF16)
            oq[s][c] = m1[s][GDN_CHUNK:]
            vn[s][c] = vnew
            kdec = (kf32[s][rs] * jnp.exp(tot - cumc[s][rs])).astype(BF16)
            st[s] = st[s] * jnp.exp(tot) + _dot_tn(kdec, vnew)

    for s in streams:
        d, h = s
        s_ref[d * N_HEADS + h] = st[s]
        o = jnp.concatenate(oq[s], axis=0) + _dot(qkm[s], jnp.concatenate(vn[s], axis=0))
        refs[d][5][0, :, h * D_HEAD:(h + 1) * D_HEAD] = o.astype(of_ref.dtype)


def _gdn(qkv, gcol, grow):
    b, l, _ = qkv.shape
    t = SEQ_BLOCK
    nb = l // t
    fwd = lambda c: pl.BlockSpec((1, t, HEADS_W), lambda bi, i: (bi, i, c))
    bwd = lambda c: pl.BlockSpec((1, t, HEADS_W), lambda bi, i: (bi, nb - 1 - i, c))
    in_specs = [
        fwd(0), fwd(1), fwd(2),
        pl.BlockSpec((1, t, 4 * N_HEADS), lambda bi, i: (bi, i, 0)),
        pl.BlockSpec((1, 2 * N_HEADS, t), lambda bi, i: (bi, 0, i)),
        bwd(0), bwd(1), bwd(2),
        pl.BlockSpec((1, t, 4 * N_HEADS), lambda bi, i: (bi, nb - 1 - i, 0)),
        pl.BlockSpec((1, 2 * N_HEADS, t), lambda bi, i: (bi, 0, nb - 1 - i)),
    ]
    out = jax.ShapeDtypeStruct((b, l, HEADS_W), BF16)
    return pl.pallas_call(
        _gdn_kernel,
        grid=(b, nb),
        in_specs=in_specs,
        out_specs=[fwd(0), bwd(0)],
        out_shape=[out, out],
        scratch_shapes=[pltpu.VMEM((2 * N_HEADS, D_HEAD, D_HEAD), F32)],
        compiler_params=pltpu.CompilerParams(
            dimension_semantics=("parallel", "arbitrary"), vmem_limit_bytes=VMEM_LIMIT),
        name="gdn",
    )(qkv, qkv, qkv, gcol, grow, qkv, qkv, qkv, gcol, grow)


def _hgrn_kernel(qf, lff, vf, qb, lfb, vb, of_ref, ob_ref, s_ref):
    @pl.when(pl.program_id(1) == 0)
    def _():
        s_ref[...] = jnp.zeros_like(s_ref)

    t = qf.shape[1]
    nc = t // HG_CHUNK
    refs = ((qf, lff, vf, of_ref), (qb, lfb, vb, ob_ref))
    streams = [(d, h) for h in range(N_HEADS) for d in (0, 1)]

    row = lax.broadcasted_iota(jnp.int32, (t, t), 0)
    col = lax.broadcasted_iota(jnp.int32, (t, t), 1)
    same = (row ^ col) < HG_CHUNK
    mask = (same & (col <= row), same & (col >= row))

    v, qdec, kdec, last, o_intra = {}, {}, {}, {}, {}
    for s in streams:
        d, h = s
        q_ref, lf_ref, v_ref, _ = refs[d]
        hs = slice(h * D_HEAD, (h + 1) * D_HEAD)
        q = q_ref[0, :, hs].astype(F32)
        v[s] = v_ref[0, :, hs]
        lf = lf_ref[0, :, d * HEADS_W + h * D_HEAD:d * HEADS_W + (h + 1) * D_HEAD]
        pre = _seg_scan(lf, HG_CHUNK, 0, False)
        tot_c = pre.reshape(nc, HG_CHUNK, D_HEAD)[:, HG_CHUNK - 1:HG_CHUNK, :]
        tot = jnp.broadcast_to(tot_c, (nc, HG_CHUNK, D_HEAD)).reshape(t, D_HEAD)
        cum, rest = (pre, tot - pre) if d == 0 else (tot - pre + lf, pre - lf)
        kk = 1.0 - jnp.exp(lf)
        qdec[s] = (q * jnp.exp(cum)).astype(BF16)
        kinv = (kk * jnp.exp(-cum)).astype(BF16)
        kdec[s] = (kk * jnp.exp(rest)).astype(BF16)
        last[s] = jnp.exp(tot_c)
        attn = jnp.where(mask[d], _dot_nt(qdec[s], kinv), 0.0).astype(BF16)
        o_intra[s] = _dot(attn, v[s])

    st = {s: s_ref[s[0] * N_HEADS + s[1]] for s in streams}
    oi = {s: [None] * nc for s in streams}
    for j in range(nc):
        for s in streams:
            c = j if s[0] == 0 else nc - 1 - j
            rs = slice(c * HG_CHUNK, (c + 1) * HG_CHUNK)
            oi[s][c] = _dot_nt(qdec[s][rs], st[s].astype(BF16))
            st[s] = st[s] * last[s][c] + _dot_tn(v[s][rs], kdec[s][rs])
    for s in streams:
        d, h = s
        s_ref[d * N_HEADS + h] = st[s]
        o = o_intra[s] + jnp.concatenate(oi[s], axis=0)
        refs[d][3][0, :, h * D_HEAD:(h + 1) * D_HEAD] = o.astype(of_ref.dtype)


def _hgrn(hq, hlf, hi):
    b, l, _ = hq.shape
    t = SEQ_BLOCK
    nb = l // t
    fwd = lambda w: pl.BlockSpec((1, t, w), lambda bi, i: (bi, i, 0))
    bwd = lambda w: pl.BlockSpec((1, t, w), lambda bi, i: (bi, nb - 1 - i, 0))
    out = jax.ShapeDtypeStruct((b, l, HEADS_W), BF16)
    return pl.pallas_call(
        _hgrn_kernel,
        grid=(b, nb),
        in_specs=[fwd(HEADS_W), fwd(2 * HEADS_W), fwd(HEADS_W),
                  bwd(HEADS_W), bwd(2 * HEADS_W), bwd(HEADS_W)],
        out_specs=[fwd(HEADS_W), bwd(HEADS_W)],
        out_shape=[out, out],
        scratch_shapes=[pltpu.VMEM((2 * N_HEADS, D_HEAD, D_HEAD), F32)],
        compiler_params=pltpu.CompilerParams(
            dimension_semantics=("parallel", "arbitrary"), vmem_limit_bytes=VMEM_LIMIT),
        name="hgrn",
    )(hq, hlf, hi, hq, hlf, hi)


def _head_norm_gate(of_ref, ob_ref, nw_ref, gs_ref):
    parts = []
    for h in range(N_HEADS):
        hs = slice(h * D_HEAD, (h + 1) * D_HEAD)
        o = of_ref[:, hs].astype(F32) + ob_ref[:, hs].astype(F32)
        o = o * lax.rsqrt(jnp.mean(o * o, axis=-1, keepdims=True) + NORM_EPS) * nw_ref[...]
        parts.append((o * gs_ref[:, hs].astype(F32)).astype(BF16))
    return jnp.concatenate(parts, axis=1)


def _mix_out_kernel(x_ref, af_ref, ab_ref, zs_ref, bf_ref, bb_ref, gs_ref, gate_ref,
                    gnw_ref, hnw_ref, wa_ref, wb_ref, wo_ref, npost_ref, y_ref):
    oa = _head_norm_gate(af_ref, ab_ref, gnw_ref, zs_ref)
    ob = _head_norm_gate(bf_ref, bb_ref, hnw_ref, gs_ref)
    ya = _dot(oa, wa_ref[...])
    yb = _dot(ob, wb_ref[...])
    merged = (gate_ref[:, 0:D_MODEL].astype(F32) * ya
              + gate_ref[:, D_MODEL:].astype(F32) * yb).astype(BF16)
    r = _dot(merged, wo_ref[...])
    r = r * lax.rsqrt(jnp.mean(r * r, axis=-1, keepdims=True) + NORM_EPS) * npost_ref[...]
    y_ref[...] = x_ref[...] + r


def _mix_out(x2, af, ab, zs, bf, bb, gs, gate, p, tm):
    m, d = x2.shape
    row = lambda w: pl.BlockSpec((tm, w), lambda i: (i, 0))
    return pl.pallas_call(
        _mix_out_kernel,
        grid=(m // tm,),
        in_specs=[row(d)] + [row(HEADS_W)] * 6 + [row(2 * d)] + [
            _const_spec(p[k].shape) for k in
            ('gdn_norm_w', 'hgrn_norm_w', 'w_br_gdn', 'w_br_hg', 'w_out', 'n_post_mix')],
        out_specs=row(d),
        out_shape=jax.ShapeDtypeStruct((m, d), F32),
        compiler_params=pltpu.CompilerParams(
            dimension_semantics=("parallel",), vmem_limit_bytes=VMEM_LIMIT),
        name="mix_out",
    )(x2, af, ab, zs, bf, bb, gs, gate, p['gdn_norm_w'], p['hgrn_norm_w'], p['w_br_gdn'],
      p['w_br_hg'], p['w_out'], p['n_post_mix'])


def _ffn_kernel(x_ref, npre_ref, wg_ref, wu_ref, wd_ref, npost_ref, y_ref):
    x = x_ref[...]
    h = (x * lax.rsqrt(jnp.mean(x * x, axis=-1, keepdims=True) + NORM_EPS)
         * npre_ref[...]).astype(BF16)
    act = (_silu(_dot(h, wg_ref[...])) * _dot(h, wu_ref[...])).astype(BF16)
    ff = _dot(act, wd_ref[...])
    ff = ff * lax.rsqrt(jnp.mean(ff * ff, axis=-1, keepdims=True) + NORM_EPS) * npost_ref[...]
    y_ref[...] = x + ff


def _ffn(x2, p, tm):
    m, d = x2.shape
    row = pl.BlockSpec((tm, d), lambda i: (i, 0))
    return pl.pallas_call(
        _ffn_kernel,
        grid=(m // tm,),
        in_specs=[row] + [_const_spec(p[k].shape) for k in
                          ('n_pre_ffn', 'w_ffn_gate', 'w_ffn_up', 'w_ffn_down', 'n_post_ffn')],
        out_specs=row,
        out_shape=jax.ShapeDtypeStruct((m, d), F32),
        compiler_params=pltpu.CompilerParams(
            dimension_semantics=("parallel",), vmem_limit_bytes=VMEM_LIMIT),
        name="ffn",
    )(x2, p['n_pre_ffn'], p['w_ffn_gate'], p['w_ffn_up'], p['w_ffn_down'], p['n_post_ffn'])


def _layer_params(l, w_in, conv_w, a_log, dt_bias, gdn_norm_w, lb_all, hgrn_norm_w, w_br_gdn,
                  w_br_hg, w_out, n_pre_mix, n_post_mix, n_pre_ffn, n_post_ffn, w_gate, w_up,
                  w_down):
    wi = w_in[l]
    o = 0
    parts = {}
    for name, width in (('w_qkv', 3 * HEADS_W), ('w_z', HEADS_W), ('ab', 4 * N_HEADS),
                        ('w_hq', HEADS_W), ('w_hf', 2 * HEADS_W), ('w_hi', HEADS_W),
                        ('w_hg', HEADS_W), ('w_gate', 2 * D_MODEL)):
        parts[name] = wi[:, o:o + width].astype(BF16)
        o += width
    ab = parts.pop('ab')
    parts['w_ab'] = jnp.pad(ab, ((0, 0), (0, LANES - 4 * N_HEADS)))
    parts['w_abt'] = ab.T
    gpar = jnp.stack([a_log[l].reshape(-1), dt_bias[l].reshape(-1)]).astype(F32)
    parts['gpar_c'] = jnp.pad(gpar, ((0, 0), (0, LANES - 2 * N_HEADS)))
    parts['gpar_r'] = jnp.pad(gpar.T, ((0, 2 * N_HEADS), (0, 0)))
    vec = lambda a: a.astype(F32).reshape(1, -1)
    parts.update(
        conv_w=conv_w[l].astype(F32), lb=vec(lb_all[l]),
        gdn_norm_w=vec(gdn_norm_w[l]), hgrn_norm_w=vec(hgrn_norm_w[l]),
        w_br_gdn=w_br_gdn[l].astype(BF16), w_br_hg=w_br_hg[l].astype(BF16),
        w_out=w_out[l].astype(BF16),
        n_pre_mix=vec(n_pre_mix[l]), n_post_mix=vec(n_post_mix[l]),
        n_pre_ffn=vec(n_pre_ffn[l]), n_post_ffn=vec(n_post_ffn[l]),
        w_ffn_gate=w_gate[l].astype(BF16), w_ffn_up=w_up[l].astype(BF16),
        w_ffn_down=w_down[l].astype(BF16))
    return parts


def _layer(x, p, tm_proj=512, tm_out=256):
    b, l, d = x.shape
    qkv, zs, gcol, grow, hq, hlf, hi, hgs, gate = _in_proj(x, p, min(tm_proj, l))
    af, ab = _gdn(qkv, gcol, grow)
    bf, bb = _hgrn(hq, hlf, hi)
    flat = lambda a: a.reshape(b * l, a.shape[-1])
    x2 = _mix_out(flat(x), flat(af), flat(ab), flat(zs), flat(bf), flat(bb), flat(hgs),
                  flat(gate), p, tm_out)
    return _ffn(x2, p, tm_out).reshape(b, l, d)


def kernel(x_prompt, x_sample, w_in, conv_w, gdn_a_log, gdn_dt_bias, gdn_norm_w, hgrn_lb_logits,
           hgrn_norm_w, w_branch_gdn, w_branch_hgrn, w_out, norm_pre_mix, norm_post_mix,
           norm_pre_ffn, norm_post_ffn, w_ffn_gate, w_ffn_up, w_ffn_down):
    lb_sm = jax.nn.softmax(hgrn_lb_logits.astype(F32), axis=0)
    lb_all = jnp.cumsum(lb_sm, axis=0) - lb_sm[0:1]
    y_prompt, y_sample = x_prompt, x_sample
    for l in range(w_in.shape[0]):
        p = _layer_params(l, w_in, conv_w, gdn_a_log, gdn_dt_bias, gdn_norm_w, lb_all,
                          hgrn_norm_w, w_branch_gdn, w_branch_hgrn, w_out, norm_pre_mix,
                          norm_post_mix, norm_pre_ffn, norm_post_ffn, w_ffn_gate, w_ffn_up,
                          w_ffn_down)
        y_prompt = _layer(y_prompt, p)
        y_sample = _layer(y_sample, p)
    return (y_prompt, y_sample)
```

```python
import functools

import jax
import jax.numpy as jnp
from jax import lax
from jax.experimental import pallas as pl
from jax.experimental.pallas import tpu as pltpu

F32 = jnp.float32
BF16 = jnp.bfloat16

D_MODEL = 1024
N_HEADS = 4
D_HEAD = 128
HEADS_W = N_HEADS * D_HEAD
GDN_CONV = 5
GDN_CHUNK = 64
HG_CHUNK = 16
HG_STATE_CHUNK = 64
NORM_EPS = 1e-6
HALO = 16
LANES = 128
SEQ_BLOCK = 512
SEQ_SUB = 256
PROJ_COLS = 256
VMEM_LIMIT = 56 * 1024 * 1024


def _dot(a, b):
    return jnp.dot(a, b, preferred_element_type=F32)


def _dot_nt(a, b):
    return lax.dot_general(a, b, (((1,), (1,)), ((), ())), preferred_element_type=F32)


def _dot_tn(a, b):
    return lax.dot_general(a, b, (((0,), (0,)), ((), ())), preferred_element_type=F32)


def _sigmoid(x):
    return 1.0 / (1.0 + jnp.exp(-x))


def _sigmoid_t(x):
    return 0.5 * jnp.tanh(0.5 * x) + 0.5


def _silu(x):
    return x * _sigmoid_t(x)


def _softplus(x):
    return jnp.maximum(x, 0.0) + jnp.log(1.0 + jnp.exp(-jnp.abs(x)))


def _seg_scan(x, group, axis, reverse):
    n = x.shape[axis]
    pos = lax.broadcasted_iota(jnp.int32, x.shape, axis) & (group - 1)
    s = 1
    while s < group:
        if reverse:
            x = x + jnp.where(pos < group - s, pltpu.roll(x, n - s, axis), 0.0)
        else:
            x = x + jnp.where(pos >= s, pltpu.roll(x, s, axis), 0.0)
        s *= 2
    return x


def _const_spec(shape):
    return pl.BlockSpec(shape, lambda *_: (0,) * len(shape), pipeline_mode=pl.Buffered(1))


def _in_proj_kernel(x_ref, xp_ref, xn_ref, npre_ref, wqkv_ref, wz_ref, wab_ref, wabt_ref,
                    whq_ref, whf_ref, whi_ref, whg_ref, wgate_ref, convw_ref, gpc_ref, gpr_ref,
                    lb_ref,
                    qkv_ref, zs_ref, gcol_ref, grow_ref, hq_ref, hlf_ref, hi_ref, hgs_ref,
                    gate_ref):
    i = pl.program_id(1)
    nt = pl.num_programs(1)
    tm = x_ref.shape[1]
    nw = npre_ref[...]

    def norm(xv):
        ms = jnp.mean(xv * xv, axis=-1, keepdims=True)
        return (xv * lax.rsqrt(ms + NORM_EPS) * nw).astype(BF16)

    h = norm(x_ref[0])

    hp = norm(xp_ref[0])
    hn = norm(xn_ref[0])
    has_prev = (i > 0).astype(F32)
    has_next = (i < nt - 1).astype(F32)
    half = GDN_CONV // 2

    for c0 in range(0, 3 * HEADS_W, PROJ_COLS):
        cs = slice(c0, c0 + PROJ_COLS)
        w = wqkv_ref[:, cs]
        ext = jnp.concatenate([_dot(hp, w) * has_prev, _dot(h, w), _dot(hn, w) * has_next], axis=0)
        acc = convw_ref[half:half + 1, cs] * ext[HALO:HALO + tm]
        for j in range(GDN_CONV):
            if j != half:
                shifted = pltpu.roll(ext, (half - j) % (tm + 2 * HALO), 0)
                acc = acc + convw_ref[j:j + 1, cs] * shifted[HALO:HALO + tm]
        y = _silu(acc)
        for g0 in range(0, PROJ_COLS, D_HEAD):
            yg = y[:, g0:g0 + D_HEAD]
            if c0 + g0 < 2 * HEADS_W:
                yg = yg * lax.rsqrt(jnp.sum(yg * yg, axis=-1, keepdims=True) + NORM_EPS)
            if c0 + g0 < HEADS_W:
                yg = yg * (D_HEAD ** -0.5)
            qkv_ref[0, :, c0 + g0:c0 + g0 + D_HEAD] = yg.astype(BF16)

    ab = _dot(h, wab_ref[...])
    lane = lax.broadcasted_iota(jnp.int32, ab.shape, 1)
    gdec = -jnp.exp(gpc_ref[0:1, :]) * _softplus(ab + gpc_ref[1:2, :])
    cum = jnp.where(lane < N_HEADS, _seg_scan(gdec, GDN_CHUNK, 0, False),
                    _seg_scan(gdec, GDN_CHUNK, 0, True))
    gcol = jnp.where(lane < 2 * N_HEADS, cum, _sigmoid_t(ab))
    gcol_ref[0] = gcol
    abt = _dot_nt(wabt_ref[...], h)
    gdec_t = -jnp.exp(gpr_ref[:, 0:1]) * _softplus(abt + gpr_ref[:, 1:2])
    sub = lax.broadcasted_iota(jnp.int32, abt.shape, 0)
    cum_t = jnp.where(sub < N_HEADS, _seg_scan(gdec_t, GDN_CHUNK, 1, False),
                      _seg_scan(gdec_t, GDN_CHUNK, 1, True))
    grow_ref[0] = cum_t[0:2 * N_HEADS, :]

    def project(w_ref, out_ref, epilogue):
        for c0 in range(0, w_ref.shape[1], PROJ_COLS):
            cs = slice(c0, c0 + PROJ_COLS)
            out_ref[0, :, cs] = epilogue(_dot(h, w_ref[:, cs]), cs).astype(out_ref.dtype)

    def log_f(r, cs):
        lb = lb_ref[:, cs]
        return jnp.log(lb + (1.0 - lb) * _sigmoid(r))

    project(whf_ref, hlf_ref, log_f)
    project(wgate_ref, gate_ref, lambda r, cs: _sigmoid_t(r))
    project(whi_ref, hi_ref, lambda r, cs: r)
    project(wz_ref, zs_ref, lambda r, cs: _silu(r))
    project(whq_ref, hq_ref, lambda r, cs: _silu(r) * (D_HEAD ** -0.5))
    project(whg_ref, hgs_ref, lambda r, cs: _silu(r))


def _in_proj(x, p, tm):
    b, l, d = x.shape
    nt = l // tm
    hb = tm // HALO
    nhb = l // HALO
    grid = (b, nt)
    row = lambda w: pl.BlockSpec((1, tm, w), lambda bi, i: (bi, i, 0))
    in_specs = [
        row(d),
        pl.BlockSpec((1, HALO, d), lambda bi, i: (bi, jnp.maximum(i * hb - 1, 0), 0)),
        pl.BlockSpec((1, HALO, d), lambda bi, i: (bi, jnp.minimum((i + 1) * hb, nhb - 1), 0)),
    ] + [_const_spec(p[k].shape) for k in
         ('n_pre_mix', 'w_qkv', 'w_z', 'w_ab', 'w_abt', 'w_hq', 'w_hf', 'w_hi', 'w_hg', 'w_gate',
          'conv_w', 'gpar_c', 'gpar_r', 'lb')]
    out_shape = [
        jax.ShapeDtypeStruct((b, l, 3 * HEADS_W), BF16),
        jax.ShapeDtypeStruct((b, l, HEADS_W), BF16),
        jax.ShapeDtypeStruct((b, l, LANES), F32),
        jax.ShapeDtypeStruct((b, 2 * N_HEADS, l), F32),
        jax.ShapeDtypeStruct((b, l, HEADS_W), BF16),
        jax.ShapeDtypeStruct((b, l, 2 * HEADS_W), F32),
        jax.ShapeDtypeStruct((b, l, HEADS_W), BF16),
        jax.ShapeDtypeStruct((b, l, HEADS_W), BF16),
        jax.ShapeDtypeStruct((b, l, 2 * D_MODEL), BF16),
    ]
    out_specs = [row(3 * HEADS_W), row(HEADS_W), row(LANES),
                 pl.BlockSpec((1, 2 * N_HEADS, tm), lambda bi, i: (bi, 0, i)),
                 row(HEADS_W), row(2 * HEADS_W), row(HEADS_W), row(HEADS_W), row(2 * D_MODEL)]
    return pl.pallas_call(
        _in_proj_kernel,
        grid=grid,
        in_specs=in_specs,
        out_specs=out_specs,
        out_shape=out_shape,
        compiler_params=pltpu.CompilerParams(
            dimension_semantics=("parallel", "parallel"), vmem_limit_bytes=VMEM_LIMIT),
        name="in_proj",
    )(x, x, x, p['n_pre_mix'], p['w_qkv'], p['w_z'], p['w_ab'], p['w_abt'], p['w_hq'], p['w_hf'],
      p['w_hi'], p['w_hg'], p['w_gate'], p['conv_w'], p['gpar_c'], p['gpar_r'], p['lb'])


def _streams():
    return [(d, h) for h in range(N_HEADS) for d in (0, 1)]


def _recurrent_kernel(sub_block, *refs):
    n_in = (len(refs) - 3) // 2
    dir_refs = (refs[:n_in] + (refs[-3],), refs[n_in:2 * n_in] + (refs[-2],))
    s_ref = refs[-1]

    @pl.when(pl.program_id(1) == 0)
    def _():
        s_ref[...] = jnp.zeros_like(s_ref)

    nsub = refs[0].shape[1] // SEQ_SUB
    st = {s: s_ref[s[0] * N_HEADS + s[1]] for s in _streams()}
    for sb in range(nsub):
        rows = (pl.ds(sb * SEQ_SUB, SEQ_SUB), pl.ds((nsub - 1 - sb) * SEQ_SUB, SEQ_SUB))
        st = sub_block(dir_refs, rows, st)
    for s in _streams():
        s_ref[s[0] * N_HEADS + s[1]] = st[s]


def _gdn_sub_block(refs, rows, st):
    t = SEQ_SUB
    nc = t // GDN_CHUNK
    streams = _streams()

    same = ((lax.broadcasted_iota(jnp.int32, (t, t), 0) ^ lax.broadcasted_iota(jnp.int32, (t, t), 1))
            < GDN_CHUNK)
    wrow = lax.broadcasted_iota(jnp.int32, (GDN_CHUNK, t), 0)
    wlane = lax.broadcasted_iota(jnp.int32, (GDN_CHUNK, t), 1)
    wcol = wlane & (GDN_CHUNK - 1)
    incl = (wcol <= wrow, wcol >= wrow)
    offdiag = wcol != wrow
    eye = (wcol == wrow).astype(F32)

    def to_wide(tall):
        tall = jnp.where(same, tall, 0.0)
        out = tall[0:GDN_CHUNK]
        for c in range(1, nc):
            out = out + tall[c * GDN_CHUNK:(c + 1) * GDN_CHUNK]
        return out

    def to_tall(wide):
        return jnp.where(same, jnp.concatenate([wide] * nc, axis=0), 0.0)

    def col_to_wide(colv):
        out = jnp.broadcast_to(colv[0:GDN_CHUNK], (GDN_CHUNK, t))
        for c in range(1, nc):
            out = jnp.where(wlane >= c * GDN_CHUNK, colv[c * GDN_CHUNK:(c + 1) * GDN_CHUNK], out)
        return out

    q, k, v, cumc, beta, a, qkm = {}, {}, {}, {}, {}, {}, {}
    for s in streams:
        d, h = s
        q_ref, k_ref, v_ref, gc_ref, gr_ref, _ = refs[d]
        hs = slice(h * D_HEAD, (h + 1) * D_HEAD)
        li = d * N_HEADS + h
        q[s] = q_ref[0, rows[d], hs]
        k[s] = k_ref[0, rows[d], hs]
        v[s] = v_ref[0, rows[d], hs]
        cumc[s] = gc_ref[0, rows[d], li:li + 1]
        beta[s] = gc_ref[0, rows[d], 2 * N_HEADS + li:2 * N_HEADS + li + 1]
        cumr = gr_ref[0, li:li + 1, rows[d]]
        dec = jnp.where(incl[d], jnp.exp(jnp.where(incl[d], col_to_wide(cumc[s]) - cumr, 0.0)),
                        0.0)
        a[s] = (jnp.where(offdiag, to_wide(_dot_nt(k[s], k[s])) * dec, 0.0)
                * col_to_wide(beta[s]))
        qkm[s] = to_tall((to_wide(_dot_nt(q[s], k[s])) * dec).astype(BF16))

    tinv = {s: eye - a[s] for s in streams}
    pw = {s: a[s].astype(BF16) for s in streams}
    span = 2
    while span < GDN_CHUNK:
        for s in streams:
            pw[s] = _dot(pw[s], to_tall(pw[s])).astype(BF16)
        for s in streams:
            tinv[s] = tinv[s] + _dot(tinv[s].astype(BF16), to_tall(pw[s]))
        span *= 2

    u, w, qdec, kf32 = {}, {}, {}, {}
    for s in streams:
        e = jnp.exp(cumc[s])
        kf32[s] = k[s].astype(F32)
        rhs = jnp.concatenate([(v[s].astype(F32) * beta[s]).astype(BF16),
                               (kf32[s] * (beta[s] * e)).astype(BF16)], axis=1)
        uw = _dot(to_tall(tinv[s].astype(BF16)), rhs)
        u[s] = uw[:, :D_HEAD]
        w[s] = uw[:, D_HEAD:].astype(BF16)
        qdec[s] = (q[s].astype(F32) * e).astype(BF16)

    st = dict(st)
    oq = {s: [None] * nc for s in streams}
    vn = {s: [None] * nc for s in streams}
    for j in range(nc):
        m1 = {}
        for s in streams:
            c = j if s[0] == 0 else nc - 1 - j
            rs = slice(c * GDN_CHUNK, (c + 1) * GDN_CHUNK)
            m1[s] = _dot(jnp.concatenate([w[s][rs], qdec[s][rs]], axis=0), st[s].astype(BF16))
        for s in streams:
            c = j if s[0] == 0 else nc - 1 - j
            rs = slice(c * GDN_CHUNK, (c + 1) * GDN_CHUNK)
            last_row = (c + 1) * GDN_CHUNK - 1 if s[0] == 0 else c * GDN_CHUNK
            tot = cumc[s][last_row:last_row + 1, :]
            vnew = (u[s][rs] - m1[s][:GDN_CHUNK]).astype(BF16)
            oq[s][c] = m1[s][GDN_CHUNK:]
            vn[s][c] = vnew
            kdec = (kf32[s][rs] * jnp.exp(tot - cumc[s][rs])).astype(BF16)
            st[s] = st[s] * jnp.exp(tot) + _dot_tn(kdec, vnew)

    for s in streams:
        d, h = s
        o = jnp.concatenate(oq[s], axis=0) + _dot(qkm[s], jnp.concatenate(vn[s], axis=0))
        o_ref = refs[d][5]
        o_ref[0, rows[d], h * D_HEAD:(h + 1) * D_HEAD] = o.astype(o_ref.dtype)
    return st


def _gdn(qkv, gcol, grow):
    b, l, _ = qkv.shape
    t = SEQ_BLOCK
    nb = l // t
    fwd = lambda c: pl.BlockSpec((1, t, HEADS_W), lambda bi, i: (bi, i, c))
    bwd = lambda c: pl.BlockSpec((1, t, HEADS_W), lambda bi, i: (bi, nb - 1 - i, c))
    in_specs = [
        fwd(0), fwd(1), fwd(2),
        pl.BlockSpec((1, t, LANES), lambda bi, i: (bi, i, 0)),
        pl.BlockSpec((1, 2 * N_HEADS, t), lambda bi, i: (bi, 0, i)),
        bwd(0), bwd(1), bwd(2),
        pl.BlockSpec((1, t, LANES), lambda bi, i: (bi, nb - 1 - i, 0)),
        pl.BlockSpec((1, 2 * N_HEADS, t), lambda bi, i: (bi, 0, nb - 1 - i)),
    ]
    out = jax.ShapeDtypeStruct((b, l, HEADS_W), BF16)
    return pl.pallas_call(
        functools.partial(_recurrent_kernel, _gdn_sub_block),
        grid=(b, nb),
        in_specs=in_specs,
        out_specs=[fwd(0), bwd(0)],
        out_shape=[out, out],
        scratch_shapes=[pltpu.VMEM((2 * N_HEADS, D_HEAD, D_HEAD), F32)],
        compiler_params=pltpu.CompilerParams(
            dimension_semantics=("parallel", "arbitrary"), vmem_limit_bytes=VMEM_LIMIT),
        name="gdn",
    )(qkv, qkv, qkv, gcol, grow, qkv, qkv, qkv, gcol, grow)


def _hgrn_sub_block(refs, rows, st):
    t = SEQ_SUB
    nsc = t // HG_CHUNK
    per = HG_STATE_CHUNK // HG_CHUNK
    nc = t // HG_STATE_CHUNK
    streams = _streams()

    row = lax.broadcasted_iota(jnp.int32, (t, t), 0)
    col = lax.broadcasted_iota(jnp.int32, (t, t), 1)
    same = (row ^ col) < HG_STATE_CHUNK
    ahead = (row & -HG_CHUNK) - (col & -HG_CHUNK)
    masks = tuple(
        [(dist == 0) & causal] + [same & (dist == k * HG_CHUNK) for k in range(1, per)]
        for dist, causal in ((ahead, col <= row), (-ahead, col >= row)))

    def expand(pieces):
        return jnp.concatenate([jnp.broadcast_to(p, (HG_CHUNK, D_HEAD)) for p in pieces], axis=0)

    v, qdec_st, kdec_st, last, o_intra = {}, {}, {}, {}, {}
    for s in streams:
        d, h = s
        q_ref, lf_ref, v_ref, _ = refs[d]
        hs = slice(h * D_HEAD, (h + 1) * D_HEAD)
        q = q_ref[0, rows[d], hs].astype(F32)
        v[s] = v_ref[0, rows[d], hs]
        lf = lf_ref[0, rows[d], d * HEADS_W + h * D_HEAD:d * HEADS_W + (h + 1) * D_HEAD]
        pre = _seg_scan(lf, HG_CHUNK, 0, False)
        tots = [pre[(j + 1) * HG_CHUNK - 1:(j + 1) * HG_CHUNK] for j in range(nsc)]
        tot = expand(tots)
        cum, rest = (pre, tot - pre) if d == 0 else (tot - pre + lf, pre - lf)
        kk = 1.0 - jnp.exp(lf)
        qd = q * jnp.exp(cum)
        kd = kk * jnp.exp(rest)
        qdec = qd.astype(BF16)
        kinv = (kk * jnp.exp(-cum)).astype(BF16)

        step = 1 if d == 0 else -1
        zero = jnp.zeros_like(tots[0])
        before, after, gaps = [], [], [[] for _ in range(per - 2)]
        for j in range(nsc):
            base, i = (j // per) * per, j % per
            group = tots[base:base + per]
            earlier = group[:i] if d == 0 else group[i + 1:]
            later = group[i + 1:] if d == 0 else group[:i]
            before.append(sum(earlier, zero))
            after.append(sum(later, zero))
            for k in range(per - 2):
                between = [group[i + step * m] for m in range(1, k + 2) if 0 <= i + step * m < per]
                gaps[k].append(sum(between, zero))
        qdec_st[s] = (qd * expand([jnp.exp(x) for x in before])).astype(BF16)
        kdec_st[s] = (kd * expand([jnp.exp(x) for x in after])).astype(BF16)
        last[s] = [jnp.exp(sum(tots[c * per:(c + 1) * per], zero)) for c in range(nc)]
        keys = [kinv, kd.astype(BF16)] + [
            (kd * expand([jnp.exp(x) for x in g])).astype(BF16) for g in gaps]
        attn = jnp.zeros((t, t), F32)
        for m, key in zip(masks[d], keys):
            attn = jnp.where(m, _dot_nt(qdec, key), attn)
        o_intra[s] = _dot(attn.astype(BF16), v[s])

    st = dict(st)
    oi = {s: [None] * nc for s in streams}
    for j in range(nc):
        for s in streams:
            c = j if s[0] == 0 else nc - 1 - j
            rs = slice(c * HG_STATE_CHUNK, (c + 1) * HG_STATE_CHUNK)
            oi[s][c] = _dot_nt(qdec_st[s][rs], st[s].astype(BF16))
            st[s] = st[s] * last[s][c] + _dot_tn(v[s][rs], kdec_st[s][rs])
    for s in streams:
        d, h = s
        o = o_intra[s] + jnp.concatenate(oi[s], axis=0)
        o_ref = refs[d][3]
        o_ref[0, rows[d], h * D_HEAD:(h + 1) * D_HEAD] = o.astype(o_ref.dtype)
    return st


def _hgrn(hq, hlf, hi):
    b, l, _ = hq.shape
    t = SEQ_BLOCK
    nb = l // t
    fwd = lambda w: pl.BlockSpec((1, t, w), lambda bi, i: (bi, i, 0))
    bwd = lambda w: pl.BlockSpec((1, t, w), lambda bi, i: (bi, nb - 1 - i, 0))
    out = jax.ShapeDtypeStruct((b, l, HEADS_W), BF16)
    return pl.pallas_call(
        functools.partial(_recurrent_kernel, _hgrn_sub_block),
        grid=(b, nb),
        in_specs=[fwd(HEADS_W), fwd(2 * HEADS_W), fwd(HEADS_W),
                  bwd(HEADS_W), bwd(2 * HEADS_W), bwd(HEADS_W)],
        out_specs=[fwd(HEADS_W), bwd(HEADS_W)],
        out_shape=[out, out],
        scratch_shapes=[pltpu.VMEM((2 * N_HEADS, D_HEAD, D_HEAD), F32)],
        compiler_params=pltpu.CompilerParams(
            dimension_semantics=("parallel", "arbitrary"), vmem_limit_bytes=VMEM_LIMIT),
        name="hgrn",
    )(hq, hlf, hi, hq, hlf, hi)


def _head_norm_gate(of_ref, ob_ref, nw_ref, gs_ref):
    parts = []
    for h in range(N_HEADS):
        hs = slice(h * D_HEAD, (h + 1) * D_HEAD)
        o = of_ref[:, hs].astype(F32) + ob_ref[:, hs].astype(F32)
        o = o * lax.rsqrt(jnp.mean(o * o, axis=-1, keepdims=True) + NORM_EPS) * nw_ref[...]
        parts.append((o * gs_ref[:, hs].astype(F32)).astype(BF16))
    return jnp.concatenate(parts, axis=1)


def _mix_out_kernel(x_ref, af_ref, ab_ref, zs_ref, bf_ref, bb_ref, gs_ref, gate_ref,
                    gnw_ref, hnw_ref, wa_ref, wb_ref, wo_ref, npost_ref, y_ref):
    oa = _head_norm_gate(af_ref, ab_ref, gnw_ref, zs_ref)
    ob = _head_norm_gate(bf_ref, bb_ref, hnw_ref, gs_ref)
    ya = _dot(oa, wa_ref[...])
    yb = _dot(ob, wb_ref[...])
    merged = (gate_ref[:, 0:D_MODEL].astype(F32) * ya
              + gate_ref[:, D_MODEL:].astype(F32) * yb).astype(BF16)
    r = _dot(merged, wo_ref[...])
    r = r * lax.rsqrt(jnp.mean(r * r, axis=-1, keepdims=True) + NORM_EPS) * npost_ref[...]
    y_ref[...] = x_ref[...] + r


def _mix_out(x2, af, ab, zs, bf, bb, gs, gate, p, tm):
    m, d = x2.shape
    row = lambda w: pl.BlockSpec((tm, w), lambda i: (i, 0))
    return pl.pallas_call(
        _mix_out_kernel,
        grid=(m // tm,),
        in_specs=[row(d)] + [row(HEADS_W)] * 6 + [row(2 * d)] + [
            _const_spec(p[k].shape) for k in
            ('gdn_norm_w', 'hgrn_norm_w', 'w_br_gdn', 'w_br_hg', 'w_out', 'n_post_mix')],
        out_specs=row(d),
        out_shape=jax.ShapeDtypeStruct((m, d), F32),
        compiler_params=pltpu.CompilerParams(
            dimension_semantics=("parallel",), vmem_limit_bytes=VMEM_LIMIT),
        name="mix_out",
    )(x2, af, ab, zs, bf, bb, gs, gate, p['gdn_norm_w'], p['hgrn_norm_w'], p['w_br_gdn'],
      p['w_br_hg'], p['w_out'], p['n_post_mix'])


def _ffn_kernel(x_ref, npre_ref, wg_ref, wu_ref, wd_ref, npost_ref, y_ref):
    x = x_ref[...]
    h = (x * lax.rsqrt(jnp.mean(x * x, axis=-1, keepdims=True) + NORM_EPS)
         * npre_ref[...]).astype(BF16)
    act = (_silu(_dot(h, wg_ref[...])) * _dot(h, wu_ref[...])).astype(BF16)
    ff = _dot(act, wd_ref[...])
    ff = ff * lax.rsqrt(jnp.mean(ff * ff, axis=-1, keepdims=True) + NORM_EPS) * npost_ref[...]
    y_ref[...] = x + ff


def _ffn(x2, p, tm):
    m, d = x2.shape
    row = pl.BlockSpec((tm, d), lambda i: (i, 0))
    return pl.pallas_call(
        _ffn_kernel,
        grid=(m // tm,),
        in_specs=[row] + [_const_spec(p[k].shape) for k in
                          ('n_pre_ffn', 'w_ffn_gate', 'w_ffn_up', 'w_ffn_down', 'n_post_ffn')],
        out_specs=row,
        out_shape=jax.ShapeDtypeStruct((m, d), F32),
        compiler_params=pltpu.CompilerParams(
            dimension_semantics=("parallel",), vmem_limit_bytes=VMEM_LIMIT),
        name="ffn",
    )(x2, p['n_pre_ffn'], p['w_ffn_gate'], p['w_ffn_up'], p['w_ffn_down'], p['n_post_ffn'])


def _layer_params(l, w_in, conv_w, a_log, dt_bias, gdn_norm_w, lb_all, hgrn_norm_w, w_br_gdn,
                  w_br_hg, w_out, n_pre_mix, n_post_mix, n_pre_ffn, n_post_ffn, w_gate, w_up,
                  w_down):
    wi = w_in[l]
    o = 0
    parts = {}
    for name, width in (('w_qkv', 3 * HEADS_W), ('w_z', HEADS_W), ('ab', 4 * N_HEADS),
                        ('w_hq', HEADS_W), ('w_hf', 2 * HEADS_W), ('w_hi', HEADS_W),
                        ('w_hg', HEADS_W), ('w_gate', 2 * D_MODEL)):
        parts[name] = wi[:, o:o + width].astype(BF16)
        o += width
    ab = parts.pop('ab')
    parts['w_ab'] = jnp.pad(ab, ((0, 0), (0, LANES - 4 * N_HEADS)))
    parts['w_abt'] = ab.T
    gpar = jnp.stack([a_log[l].reshape(-1), dt_bias[l].reshape(-1)]).astype(F32)
    parts['gpar_c'] = jnp.pad(gpar, ((0, 0), (0, LANES - 2 * N_HEADS)))
    parts['gpar_r'] = jnp.pad(gpar.T, ((0, 2 * N_HEADS), (0, 0)))
    vec = lambda a: a.astype(F32).reshape(1, -1)
    parts.update(
        conv_w=conv_w[l].astype(F32), lb=vec(lb_all[l]),
        gdn_norm_w=vec(gdn_norm_w[l]), hgrn_norm_w=vec(hgrn_norm_w[l]),
        w_br_gdn=w_br_gdn[l].astype(BF16), w_br_hg=w_br_hg[l].astype(BF16),
        w_out=w_out[l].astype(BF16),
        n_pre_mix=vec(n_pre_mix[l]), n_post_mix=vec(n_post_mix[l]),
        n_pre_ffn=vec(n_pre_ffn[l]), n_post_ffn=vec(n_post_ffn[l]),
        w_ffn_gate=w_gate[l].astype(BF16), w_ffn_up=w_up[l].astype(BF16),
        w_ffn_down=w_down[l].astype(BF16))
    return parts


def _layer(x, p, tm_proj=512, tm_out=512):
    b, l, d = x.shape
    qkv, zs, gcol, grow, hq, hlf, hi, hgs, gate = _in_proj(x, p, min(tm_proj, l))
    af, ab = _gdn(qkv, gcol, grow)
    bf, bb = _hgrn(hq, hlf, hi)
    flat = lambda a: a.reshape(b * l, a.shape[-1])
    x2 = _mix_out(flat(x), flat(af), flat(ab), flat(zs), flat(bf), flat(bb), flat(hgs),
                  flat(gate), p, tm_out)
    return _ffn(x2, p, tm_out).reshape(b, l, d)


def kernel(x_prompt, x_sample, w_in, conv_w, gdn_a_log, gdn_dt_bias, gdn_norm_w, hgrn_lb_logits,
           hgrn_norm_w, w_branch_gdn, w_branch_hgrn, w_out, norm_pre_mix, norm_post_mix,
           norm_pre_ffn, norm_post_ffn, w_ffn_gate, w_ffn_up, w_ffn_down):
    lb_sm = jax.nn.softmax(hgrn_lb_logits.astype(F32), axis=0)
    lb_all = jnp.cumsum(lb_sm, axis=0) - lb_sm[0:1]
    y_prompt, y_sample = x_prompt, x_sample
    for l in range(w_in.shape[0]):
        p = _layer_params(l, w_in, conv_w, gdn_a_log, gdn_dt_bias, gdn_norm_w, lb_all,
                          hgrn_norm_w, w_branch_gdn, w_branch_hgrn, w_out, norm_pre_mix,
                          norm_post_mix, norm_pre_ffn, norm_post_ffn, w_ffn_gate, w_ffn_up,
                          w_ffn_down)
        y_prompt = _layer(y_prompt, p)
        y_sample = _layer(y_sample, p)
    return (y_prompt, y_sample)
```

```python
import functools

import jax
import jax.numpy as jnp
from jax import lax
from jax.experimental import pallas as pl
from jax.experimental.pallas import tpu as pltpu

F32 = jnp.float32
BF16 = jnp.bfloat16

D_MODEL = 1024
N_HEADS = 4
D_HEAD = 128
HEADS_W = N_HEADS * D_HEAD
GDN_CONV = 5
GDN_CHUNK = 64
HG_CHUNK = 16
HG_STATE_CHUNK = 64
NORM_EPS = 1e-6
HALO = 16
LANES = 128
SEQ_BLOCK = 1024
SEQ_SUB = 256
PROJ_COLS = 256
VMEM_LIMIT = 56 * 1024 * 1024


def _dot(a, b):
    return jnp.dot(a, b, preferred_element_type=F32)


def _dot_nt(a, b):
    return lax.dot_general(a, b, (((1,), (1,)), ((), ())), preferred_element_type=F32)


def _dot_tn(a, b):
    return lax.dot_general(a, b, (((0,), (0,)), ((), ())), preferred_element_type=F32)


def _sigmoid(x):
    return 1.0 / (1.0 + jnp.exp(-x))


def _sigmoid_t(x):
    return 0.5 * jnp.tanh(0.5 * x) + 0.5


def _silu(x):
    return x * _sigmoid_t(x)


def _sigmoid_half(hx):
    return 0.5 * jnp.tanh(hx) + 0.5


def _silu_half(hx):
    return hx + hx * jnp.tanh(hx)


def _softplus(x):
    return jnp.maximum(x, 0.0) + jnp.log(1.0 + jnp.exp(-jnp.abs(x)))


def _seg_scan(x, group, axis, reverse):
    n = x.shape[axis]
    pos = lax.broadcasted_iota(jnp.int32, x.shape, axis) & (group - 1)
    s = 1
    while s < group:
        if reverse:
            x = x + jnp.where(pos < group - s, pltpu.roll(x, n - s, axis), 0.0)
        else:
            x = x + jnp.where(pos >= s, pltpu.roll(x, s, axis), 0.0)
        s *= 2
    return x


def _const_spec(shape):
    return pl.BlockSpec(shape, lambda *_: (0,) * len(shape), pipeline_mode=pl.Buffered(1))


def _in_proj_kernel(x_ref, xp_ref, xn_ref, npre_ref, wqkv_ref, wz_ref, wab_ref, wabt_ref,
                    whq_ref, whf_ref, whi_ref, whg_ref, wgate_ref, convw_ref, gpc_ref, gpr_ref,
                    lb_ref,
                    qkv_ref, zs_ref, gcol_ref, grow_ref, hq_ref, hlf_ref, hi_ref, hgs_ref,
                    gate_ref):
    i = pl.program_id(1)
    nt = pl.num_programs(1)
    tm = x_ref.shape[1]
    nw = npre_ref[...]

    def norm(xv):
        ms = jnp.mean(xv * xv, axis=-1, keepdims=True)
        return (xv * lax.rsqrt(ms + NORM_EPS) * nw).astype(BF16)

    h = norm(x_ref[0])

    hp = norm(xp_ref[0])
    hn = norm(xn_ref[0])
    has_prev = (i > 0).astype(F32)
    has_next = (i < nt - 1).astype(F32)
    half = GDN_CONV // 2

    for c0 in range(0, 3 * HEADS_W, PROJ_COLS):
        cs = slice(c0, c0 + PROJ_COLS)
        w = wqkv_ref[:, cs]
        ext = jnp.concatenate([_dot(hp, w) * has_prev, _dot(h, w), _dot(hn, w) * has_next], axis=0)
        acc = convw_ref[half:half + 1, cs] * ext[HALO:HALO + tm]
        for j in range(GDN_CONV):
            if j != half:
                shifted = pltpu.roll(ext, (half - j) % (tm + 2 * HALO), 0)
                acc = acc + convw_ref[j:j + 1, cs] * shifted[HALO:HALO + tm]
        y = _silu_half(acc)
        for g0 in range(0, PROJ_COLS, D_HEAD):
            yg = y[:, g0:g0 + D_HEAD]
            if c0 + g0 < 2 * HEADS_W:
                yg = yg * lax.rsqrt(jnp.sum(yg * yg, axis=-1, keepdims=True) + NORM_EPS)
            if c0 + g0 < HEADS_W:
                yg = yg * (D_HEAD ** -0.5)
            qkv_ref[0, :, c0 + g0:c0 + g0 + D_HEAD] = yg.astype(BF16)

    ab = _dot(h, wab_ref[...])
    lane = lax.broadcasted_iota(jnp.int32, ab.shape, 1)
    gdec = -jnp.exp(gpc_ref[0:1, :]) * _softplus(ab + gpc_ref[1:2, :])
    cum = jnp.where(lane < N_HEADS, _seg_scan(gdec, GDN_CHUNK, 0, False),
                    _seg_scan(gdec, GDN_CHUNK, 0, True))
    gcol = jnp.where(lane < 2 * N_HEADS, cum, _sigmoid_t(ab))
    gcol_ref[0] = gcol
    abt = _dot_nt(wabt_ref[...], h)
    gdec_t = -jnp.exp(gpr_ref[:, 0:1]) * _softplus(abt + gpr_ref[:, 1:2])
    sub = lax.broadcasted_iota(jnp.int32, abt.shape, 0)
    cum_t = jnp.where(sub < N_HEADS, _seg_scan(gdec_t, GDN_CHUNK, 1, False),
                      _seg_scan(gdec_t, GDN_CHUNK, 1, True))
    grow_ref[0] = cum_t[0:2 * N_HEADS, :]

    def project(w_ref, out_ref, epilogue):
        for c0 in range(0, w_ref.shape[1], PROJ_COLS):
            cs = slice(c0, c0 + PROJ_COLS)
            out_ref[0, :, cs] = epilogue(_dot(h, w_ref[:, cs]), cs).astype(out_ref.dtype)

    def log_f(r, cs):
        lb = lb_ref[:, cs]
        return jnp.log(lb + (1.0 - lb) * _sigmoid(r))

    project(whf_ref, hlf_ref, log_f)
    project(wgate_ref, gate_ref, lambda r, cs: _sigmoid_half(r))
    project(whi_ref, hi_ref, lambda r, cs: r)
    project(wz_ref, zs_ref, lambda r, cs: _silu_half(r))
    project(whq_ref, hq_ref, lambda r, cs: _silu_half(r) * (D_HEAD ** -0.5))
    project(whg_ref, hgs_ref, lambda r, cs: _silu_half(r))


def _in_proj(x, p, tm):
    b, l, d = x.shape
    nt = l // tm
    hb = tm // HALO
    nhb = l // HALO
    grid = (b, nt)
    row = lambda w: pl.BlockSpec((1, tm, w), lambda bi, i: (bi, i, 0))
    in_specs = [
        row(d),
        pl.BlockSpec((1, HALO, d), lambda bi, i: (bi, jnp.maximum(i * hb - 1, 0), 0)),
        pl.BlockSpec((1, HALO, d), lambda bi, i: (bi, jnp.minimum((i + 1) * hb, nhb - 1), 0)),
    ] + [_const_spec(p[k].shape) for k in
         ('n_pre_mix', 'w_qkv', 'w_z', 'w_ab', 'w_abt', 'w_hq', 'w_hf', 'w_hi', 'w_hg', 'w_gate',
          'conv_w', 'gpar_c', 'gpar_r', 'lb')]
    out_shape = [
        jax.ShapeDtypeStruct((b, l, 3 * HEADS_W), BF16),
        jax.ShapeDtypeStruct((b, l, HEADS_W), BF16),
        jax.ShapeDtypeStruct((b, l, LANES), F32),
        jax.ShapeDtypeStruct((b, 2 * N_HEADS, l), F32),
        jax.ShapeDtypeStruct((b, l, HEADS_W), BF16),
        jax.ShapeDtypeStruct((b, l, 2 * HEADS_W), F32),
        jax.ShapeDtypeStruct((b, l, HEADS_W), BF16),
        jax.ShapeDtypeStruct((b, l, HEADS_W), BF16),
        jax.ShapeDtypeStruct((b, l, 2 * D_MODEL), BF16),
    ]
    out_specs = [row(3 * HEADS_W), row(HEADS_W), row(LANES),
                 pl.BlockSpec((1, 2 * N_HEADS, tm), lambda bi, i: (bi, 0, i)),
                 row(HEADS_W), row(2 * HEADS_W), row(HEADS_W), row(HEADS_W), row(2 * D_MODEL)]
    return pl.pallas_call(
        _in_proj_kernel,
        grid=grid,
        in_specs=in_specs,
        out_specs=out_specs,
        out_shape=out_shape,
        compiler_params=pltpu.CompilerParams(
            dimension_semantics=("parallel", "parallel"), vmem_limit_bytes=VMEM_LIMIT),
        name="in_proj",
    )(x, x, x, p['n_pre_mix'], p['w_qkv'], p['w_z'], p['w_ab'], p['w_abt'], p['w_hq'], p['w_hf'],
      p['w_hi'], p['w_hg'], p['w_gate'], p['conv_w'], p['gpar_c'], p['gpar_r'], p['lb'])


def _streams():
    return [(d, h) for h in range(N_HEADS) for d in (0, 1)]


def _recurrent_kernel(sub_block, *refs):
    n_in = (len(refs) - 3) // 2
    dir_refs = (refs[:n_in] + (refs[-3],), refs[n_in:2 * n_in] + (refs[-2],))
    s_ref = refs[-1]

    @pl.when(pl.program_id(1) == 0)
    def _():
        s_ref[...] = jnp.zeros_like(s_ref)

    nsub = refs[0].shape[1] // SEQ_SUB
    st = {s: s_ref[s[0] * N_HEADS + s[1]] for s in _streams()}
    for sb in range(nsub):
        rows = (pl.ds(sb * SEQ_SUB, SEQ_SUB), pl.ds((nsub - 1 - sb) * SEQ_SUB, SEQ_SUB))
        st = sub_block(dir_refs, rows, st)
    for s in _streams():
        s_ref[s[0] * N_HEADS + s[1]] = st[s]


def _gdn_sub_block(refs, rows, st):
    t = SEQ_SUB
    nc = t // GDN_CHUNK
    streams = _streams()

    same = ((lax.broadcasted_iota(jnp.int32, (t, t), 0) ^ lax.broadcasted_iota(jnp.int32, (t, t), 1))
            < GDN_CHUNK)
    wrow = lax.broadcasted_iota(jnp.int32, (GDN_CHUNK, t), 0)
    wlane = lax.broadcasted_iota(jnp.int32, (GDN_CHUNK, t), 1)
    wcol = wlane & (GDN_CHUNK - 1)
    incl = (wcol <= wrow, wcol >= wrow)
    offdiag = wcol != wrow
    eye = (wcol == wrow).astype(F32)

    def to_wide(tall):
        tall = jnp.where(same, tall, 0.0)
        out = tall[0:GDN_CHUNK]
        for c in range(1, nc):
            out = out + tall[c * GDN_CHUNK:(c + 1) * GDN_CHUNK]
        return out

    def to_tall(wide):
        return jnp.where(same, jnp.concatenate([wide] * nc, axis=0), 0.0)

    def col_to_wide(colv):
        out = jnp.broadcast_to(colv[0:GDN_CHUNK], (GDN_CHUNK, t))
        for c in range(1, nc):
            out = jnp.where(wlane >= c * GDN_CHUNK, colv[c * GDN_CHUNK:(c + 1) * GDN_CHUNK], out)
        return out

    q, k, v, cumc, beta, a, qkm = {}, {}, {}, {}, {}, {}, {}
    for s in streams:
        d, h = s
        q_ref, k_ref, v_ref, gc_ref, gr_ref, _ = refs[d]
        hs = slice(h * D_HEAD, (h + 1) * D_HEAD)
        li = d * N_HEADS + h
        q[s] = q_ref[0, rows[d], hs]
        k[s] = k_ref[0, rows[d], hs]
        v[s] = v_ref[0, rows[d], hs]
        cumc[s] = gc_ref[0, rows[d], li:li + 1]
        beta[s] = gc_ref[0, rows[d], 2 * N_HEADS + li:2 * N_HEADS + li + 1]
        cumr = gr_ref[0, li:li + 1, rows[d]]
        dec = jnp.where(incl[d], jnp.exp(jnp.where(incl[d], col_to_wide(cumc[s]) - cumr, 0.0)),
                        0.0)
        a[s] = (jnp.where(offdiag, to_wide(_dot_nt(k[s], k[s])) * dec, 0.0)
                * col_to_wide(beta[s]))
        qkm[s] = to_tall((to_wide(_dot_nt(q[s], k[s])) * dec).astype(BF16))

    tinv = {s: eye - a[s] for s in streams}
    pw = {}
    for s in streams:
        ab = a[s].astype(BF16)
        pw[s] = _dot(ab, to_tall(ab)).astype(BF16)
    span = 2
    while 2 * span < GDN_CHUNK:
        for s in streams:
            both = _dot(jnp.concatenate([tinv[s].astype(BF16), pw[s]], axis=0), to_tall(pw[s]))
            tinv[s] = tinv[s] + both[:GDN_CHUNK]
            pw[s] = both[GDN_CHUNK:].astype(BF16)
        span *= 2
    for s in streams:
        tinv[s] = tinv[s] + _dot(tinv[s].astype(BF16), to_tall(pw[s]))

    u, w, qdec, kf32 = {}, {}, {}, {}
    for s in streams:
        e = jnp.exp(cumc[s])
        kf32[s] = k[s].astype(F32)
        rhs = jnp.concatenate([(v[s].astype(F32) * beta[s]).astype(BF16),
                               (kf32[s] * (beta[s] * e)).astype(BF16)], axis=1)
        uw = _dot(to_tall(tinv[s].astype(BF16)), rhs)
        u[s] = uw[:, :D_HEAD]
        w[s] = uw[:, D_HEAD:].astype(BF16)
        qdec[s] = (q[s].astype(F32) * e).astype(BF16)

    st = dict(st)
    oq = {s: [None] * nc for s in streams}
    vn = {s: [None] * nc for s in streams}
    for j in range(nc):
        m1 = {}
        for s in streams:
            c = j if s[0] == 0 else nc - 1 - j
            rs = slice(c * GDN_CHUNK, (c + 1) * GDN_CHUNK)
            m1[s] = _dot(jnp.concatenate([w[s][rs], qdec[s][rs]], axis=0), st[s].astype(BF16))
        for s in streams:
            c = j if s[0] == 0 else nc - 1 - j
            rs = slice(c * GDN_CHUNK, (c + 1) * GDN_CHUNK)
            last_row = (c + 1) * GDN_CHUNK - 1 if s[0] == 0 else c * GDN_CHUNK
            tot = cumc[s][last_row:last_row + 1, :]
            vnew = (u[s][rs] - m1[s][:GDN_CHUNK]).astype(BF16)
            oq[s][c] = m1[s][GDN_CHUNK:]
            vn[s][c] = vnew
            kdec = (kf32[s][rs] * jnp.exp(tot - cumc[s][rs])).astype(BF16)
            st[s] = st[s] * jnp.exp(tot) + _dot_tn(kdec, vnew)

    for s in streams:
        d, h = s
        o = jnp.concatenate(oq[s], axis=0) + _dot(qkm[s], jnp.concatenate(vn[s], axis=0))
        o_ref = refs[d][5]
        o_ref[0, rows[d], h * D_HEAD:(h + 1) * D_HEAD] = o.astype(o_ref.dtype)
    return st


def _gdn(qkv, gcol, grow):
    b, l, _ = qkv.shape
    t = SEQ_BLOCK
    nb = l // t
    fwd = lambda c: pl.BlockSpec((1, t, HEADS_W), lambda bi, i: (bi, i, c))
    bwd = lambda c: pl.BlockSpec((1, t, HEADS_W), lambda bi, i: (bi, nb - 1 - i, c))
    in_specs = [
        fwd(0), fwd(1), fwd(2),
        pl.BlockSpec((1, t, LANES), lambda bi, i: (bi, i, 0)),
        pl.BlockSpec((1, 2 * N_HEADS, t), lambda bi, i: (bi, 0, i)),
        bwd(0), bwd(1), bwd(2),
        pl.BlockSpec((1, t, LANES), lambda bi, i: (bi, nb - 1 - i, 0)),
        pl.BlockSpec((1, 2 * N_HEADS, t), lambda bi, i: (bi, 0, nb - 1 - i)),
    ]
    out = jax.ShapeDtypeStruct((b, l, HEADS_W), BF16)
    return pl.pallas_call(
        functools.partial(_recurrent_kernel, _gdn_sub_block),
        grid=(b, nb),
        in_specs=in_specs,
        out_specs=[fwd(0), bwd(0)],
        out_shape=[out, out],
        scratch_shapes=[pltpu.VMEM((2 * N_HEADS, D_HEAD, D_HEAD), F32)],
        compiler_params=pltpu.CompilerParams(
            dimension_semantics=("parallel", "arbitrary"), vmem_limit_bytes=VMEM_LIMIT),
        name="gdn",
    )(qkv, qkv, qkv, gcol, grow, qkv, qkv, qkv, gcol, grow)


def _hgrn_sub_block(refs, rows, st):
    t = SEQ_SUB
    nsc = t // HG_CHUNK
    per = HG_STATE_CHUNK // HG_CHUNK
    nc = t // HG_STATE_CHUNK
    streams = _streams()

    row = lax.broadcasted_iota(jnp.int32, (t, t), 0)
    col = lax.broadcasted_iota(jnp.int32, (t, t), 1)
    same = (row ^ col) < HG_STATE_CHUNK
    ahead = (row & -HG_CHUNK) - (col & -HG_CHUNK)
    masks = tuple(
        [(dist == 0) & causal] + [same & (dist == k * HG_CHUNK) for k in range(1, per)]
        for dist, causal in ((ahead, col <= row), (-ahead, col >= row)))

    def expand(pieces):
        return jnp.concatenate([jnp.broadcast_to(p, (HG_CHUNK, D_HEAD)) for p in pieces], axis=0)

    v, qdec_st, kdec_st, last, o_intra = {}, {}, {}, {}, {}
    for s in streams:
        d, h = s
        q_ref, lf_ref, v_ref, _ = refs[d]
        hs = slice(h * D_HEAD, (h + 1) * D_HEAD)
        q = q_ref[0, rows[d], hs].astype(F32)
        v[s] = v_ref[0, rows[d], hs]
        lf = lf_ref[0, rows[d], d * HEADS_W + h * D_HEAD:d * HEADS_W + (h + 1) * D_HEAD]
        pre = _seg_scan(lf, HG_CHUNK, 0, False)
        tots = [pre[(j + 1) * HG_CHUNK - 1:(j + 1) * HG_CHUNK] for j in range(nsc)]
        tot = expand(tots)
        cum, rest = (pre, tot - pre) if d == 0 else (tot - pre + lf, pre - lf)
        kk = 1.0 - jnp.exp(lf)
        qd = q * jnp.exp(cum)
        kd = kk * jnp.exp(rest)
        qdec = qd.astype(BF16)
        kinv = (kk * jnp.exp(-cum)).astype(BF16)

        step = 1 if d == 0 else -1
        zero = jnp.zeros_like(tots[0])
        before, after, gaps = [], [], [[] for _ in range(per - 2)]
        for j in range(nsc):
            base, i = (j // per) * per, j % per
            group = tots[base:base + per]
            earlier = group[:i] if d == 0 else group[i + 1:]
            later = group[i + 1:] if d == 0 else group[:i]
            before.append(sum(earlier, zero))
            after.append(sum(later, zero))
            for k in range(per - 2):
                between = [group[i + step * m] for m in range(1, k + 2) if 0 <= i + step * m < per]
                gaps[k].append(sum(between, zero))
        qdec_st[s] = (qd * expand([jnp.exp(x) for x in before])).astype(BF16)
        kdec_st[s] = (kd * expand([jnp.exp(x) for x in after])).astype(BF16)
        last[s] = [jnp.exp(sum(tots[c * per:(c + 1) * per], zero)) for c in range(nc)]
        keys = [kinv, kd.astype(BF16)] + [
            (kd * expand([jnp.exp(x) for x in g])).astype(BF16) for g in gaps]
        attn = jnp.zeros((t, t), F32)
        for m, key in zip(masks[d], keys):
            attn = jnp.where(m, _dot_nt(qdec, key), attn)
        o_intra[s] = _dot(attn.astype(BF16), v[s])

    st = dict(st)
    oi = {s: [None] * nc for s in streams}
    for j in range(nc):
        for s in streams:
            c = j if s[0] == 0 else nc - 1 - j
            rs = slice(c * HG_STATE_CHUNK, (c + 1) * HG_STATE_CHUNK)
            oi[s][c] = _dot_nt(qdec_st[s][rs], st[s].astype(BF16))
            st[s] = st[s] * last[s][c] + _dot_tn(v[s][rs], kdec_st[s][rs])
    for s in streams:
        d, h = s
        o = o_intra[s] + jnp.concatenate(oi[s], axis=0)
        o_ref = refs[d][3]
        o_ref[0, rows[d], h * D_HEAD:(h + 1) * D_HEAD] = o.astype(o_ref.dtype)
    return st


def _hgrn(hq, hlf, hi):
    b, l, _ = hq.shape
    t = SEQ_BLOCK
    nb = l // t
    fwd = lambda w: pl.BlockSpec((1, t, w), lambda bi, i: (bi, i, 0))
    bwd = lambda w: pl.BlockSpec((1, t, w), lambda bi, i: (bi, nb - 1 - i, 0))
    out = jax.ShapeDtypeStruct((b, l, HEADS_W), BF16)
    return pl.pallas_call(
        functools.partial(_recurrent_kernel, _hgrn_sub_block),
        grid=(b, nb),
        in_specs=[fwd(HEADS_W), fwd(2 * HEADS_W), fwd(HEADS_W),
                  bwd(HEADS_W), bwd(2 * HEADS_W), bwd(HEADS_W)],
        out_specs=[fwd(HEADS_W), bwd(HEADS_W)],
        out_shape=[out, out],
        scratch_shapes=[pltpu.VMEM((2 * N_HEADS, D_HEAD, D_HEAD), F32)],
        compiler_params=pltpu.CompilerParams(
            dimension_semantics=("parallel", "arbitrary"), vmem_limit_bytes=VMEM_LIMIT),
        name="hgrn",
    )(hq, hlf, hi, hq, hlf, hi)


def _head_norm_gate(of_ref, ob_ref, nw_ref, gs_ref):
    parts = []
    for h in range(N_HEADS):
        hs = slice(h * D_HEAD, (h + 1) * D_HEAD)
        o = of_ref[:, hs].astype(F32) + ob_ref[:, hs].astype(F32)
        o = o * lax.rsqrt(jnp.mean(o * o, axis=-1, keepdims=True) + NORM_EPS) * nw_ref[...]
        parts.append((o * gs_ref[:, hs].astype(F32)).astype(BF16))
    return jnp.concatenate(parts, axis=1)


def _mix_out_kernel(x_ref, af_ref, ab_ref, zs_ref, bf_ref, bb_ref, gs_ref, gate_ref,
                    gnw_ref, hnw_ref, wa_ref, wb_ref, wo_ref, npost_ref, y_ref):
    oa = _head_norm_gate(af_ref, ab_ref, gnw_ref, zs_ref)
    ob = _head_norm_gate(bf_ref, bb_ref, hnw_ref, gs_ref)
    ya = _dot(oa, wa_ref[...])
    yb = _dot(ob, wb_ref[...])
    merged = (gate_ref[:, 0:D_MODEL].astype(F32) * ya
              + gate_ref[:, D_MODEL:].astype(F32) * yb).astype(BF16)
    r = _dot(merged, wo_ref[...])
    r = r * lax.rsqrt(jnp.mean(r * r, axis=-1, keepdims=True) + NORM_EPS) * npost_ref[...]
    y_ref[...] = x_ref[...] + r


def _mix_out(x2, af, ab, zs, bf, bb, gs, gate, p, tm):
    m, d = x2.shape
    row = lambda w: pl.BlockSpec((tm, w), lambda i: (i, 0))
    return pl.pallas_call(
        _mix_out_kernel,
        grid=(m // tm,),
        in_specs=[row(d)] + [row(HEADS_W)] * 6 + [row(2 * d)] + [
            _const_spec(p[k].shape) for k in
            ('gdn_norm_w', 'hgrn_norm_w', 'w_br_gdn', 'w_br_hg', 'w_out', 'n_post_mix')],
        out_specs=row(d),
        out_shape=jax.ShapeDtypeStruct((m, d), F32),
        compiler_params=pltpu.CompilerParams(
            dimension_semantics=("parallel",), vmem_limit_bytes=VMEM_LIMIT),
        name="mix_out",
    )(x2, af, ab, zs, bf, bb, gs, gate, p['gdn_norm_w'], p['hgrn_norm_w'], p['w_br_gdn'],
      p['w_br_hg'], p['w_out'], p['n_post_mix'])


def _ffn_kernel(x_ref, npre_ref, wg_ref, wu_ref, wd_ref, npost_ref, y_ref):
    x = x_ref[...]
    h = (x * lax.rsqrt(jnp.mean(x * x, axis=-1, keepdims=True) + NORM_EPS)
         * npre_ref[...]).astype(BF16)
    act = (_silu(_dot(h, wg_ref[...])) * _dot(h, wu_ref[...])).astype(BF16)
    ff = _dot(act, wd_ref[...])
    ff = ff * lax.rsqrt(jnp.mean(ff * ff, axis=-1, keepdims=True) + NORM_EPS) * npost_ref[...]
    y_ref[...] = x + ff


def _ffn(x2, p, tm):
    m, d = x2.shape
    row = pl.BlockSpec((tm, d), lambda i: (i, 0))
    return pl.pallas_call(
        _ffn_kernel,
        grid=(m // tm,),
        in_specs=[row] + [_const_spec(p[k].shape) for k in
                          ('n_pre_ffn', 'w_ffn_gate', 'w_ffn_up', 'w_ffn_down', 'n_post_ffn')],
        out_specs=row,
        out_shape=jax.ShapeDtypeStruct((m, d), F32),
        compiler_params=pltpu.CompilerParams(
            dimension_semantics=("parallel",), vmem_limit_bytes=VMEM_LIMIT),
        name="ffn",
    )(x2, p['n_pre_ffn'], p['w_ffn_gate'], p['w_ffn_up'], p['w_ffn_down'], p['n_post_ffn'])


def _layer_params(l, w_in, conv_w, a_log, dt_bias, gdn_norm_w, lb_all, hgrn_norm_w, w_br_gdn,
                  w_br_hg, w_out, n_pre_mix, n_post_mix, n_pre_ffn, n_post_ffn, w_gate, w_up,
                  w_down):
    wi = w_in[l]
    o = 0
    parts = {}
    for name, width in (('w_qkv', 3 * HEADS_W), ('w_z', HEADS_W), ('ab', 4 * N_HEADS),
                        ('w_hq', HEADS_W), ('w_hf', 2 * HEADS_W), ('w_hi', HEADS_W),
                        ('w_hg', HEADS_W), ('w_gate', 2 * D_MODEL)):
        scale = 0.5 if name in ('w_z', 'w_hq', 'w_hg', 'w_gate') else 1.0
        parts[name] = (wi[:, o:o + width] * scale).astype(BF16)
        o += width
    ab = parts.pop('ab')
    parts['w_ab'] = jnp.pad(ab, ((0, 0), (0, LANES - 4 * N_HEADS)))
    parts['w_abt'] = ab.T
    gpar = jnp.stack([a_log[l].reshape(-1), dt_bias[l].reshape(-1)]).astype(F32)
    parts['gpar_c'] = jnp.pad(gpar, ((0, 0), (0, LANES - 2 * N_HEADS)))
    parts['gpar_r'] = jnp.pad(gpar.T, ((0, 2 * N_HEADS), (0, 0)))
    vec = lambda a: a.astype(F32).reshape(1, -1)
    parts.update(
        conv_w=0.5 * conv_w[l].astype(F32), lb=vec(lb_all[l]),
        gdn_norm_w=vec(gdn_norm_w[l]), hgrn_norm_w=vec(hgrn_norm_w[l]),
        w_br_gdn=w_br_gdn[l].astype(BF16), w_br_hg=w_br_hg[l].astype(BF16),
        w_out=w_out[l].astype(BF16),
        n_pre_mix=vec(n_pre_mix[l]), n_post_mix=vec(n_post_mix[l]),
        n_pre_ffn=vec(n_pre_ffn[l]), n_post_ffn=vec(n_post_ffn[l]),
        w_ffn_gate=w_gate[l].astype(BF16), w_ffn_up=w_up[l].astype(BF16),
        w_ffn_down=w_down[l].astype(BF16))
    return parts


def _layer(x, p, tm_proj=512, tm_out=512):
    b, l, d = x.shape
    qkv, zs, gcol, grow, hq, hlf, hi, hgs, gate = _in_proj(x, p, min(tm_proj, l))
    af, ab = _gdn(qkv, gcol, grow)
    bf, bb = _hgrn(hq, hlf, hi)
    flat = lambda a: a.reshape(b * l, a.shape[-1])
    x2 = _mix_out(flat(x), flat(af), flat(ab), flat(zs), flat(bf), flat(bb), flat(hgs),
                  flat(gate), p, tm_out)
    return _ffn(x2, p, tm_out).reshape(b, l, d)


def kernel(x_prompt, x_sample, w_in, conv_w, gdn_a_log, gdn_dt_bias, gdn_norm_w, hgrn_lb_logits,
           hgrn_norm_w, w_branch_gdn, w_branch_hgrn, w_out, norm_pre_mix, norm_post_mix,
           norm_pre_ffn, norm_post_ffn, w_ffn_gate, w_ffn_up, w_ffn_down):
    lb_sm = jax.nn.softmax(hgrn_lb_logits.astype(F32), axis=0)
    lb_all = jnp.cumsum(lb_sm, axis=0) - lb_sm[0:1]
    y_prompt, y_sample = x_prompt, x_sample
    for l in range(w_in.shape[0]):
        p = _layer_params(l, w_in, conv_w, gdn_a_log, gdn_dt_bias, gdn_norm_w, lb_all,
                          hgrn_norm_w, w_branch_gdn, w_branch_hgrn, w_out, norm_pre_mix,
                          norm_post_mix, norm_pre_ffn, norm_post_ffn, w_ffn_gate, w_ffn_up,
                          w_ffn_down)
        y_prompt = _layer(y_prompt, p)
        y_sample = _layer(y_sample, p)
    return (y_prompt, y_sample)
```
